```python
import math
import jax, jax.numpy as jnp
from jax import lax
import numpy as np

D_MODEL = 1024
BATCH = 8
SEQ = 4096
DEPTH = 4

N_MIXERS = 3
N_CONV_LAYERS = (DEPTH + 2) // 3
N_LRU_LAYERS = (DEPTH + 1) // 3
N_MLA_LAYERS = DEPTH // 3

CONV_WIDTH = 3

LRU_WIDTH = 1280
LRU_BLOCKS = 10
LRU_BLOCK_W = LRU_WIDTH // LRU_BLOCKS
LRU_CONV_WIDTH = 4
LRU_C = 8.0

MLA_HEADS = 8
Q_LORA_RANK = 384
KV_LORA_RANK = 256
QK_NOPE_DIM = 128
QK_ROPE_DIM = 64
V_HEAD_DIM = 128
ROPE_THETA = 10000.0
Q_BLOCK = 128

D_FF = ((8 * D_MODEL // 3 + 255) // 256) * 256

NORM_EPS = 1e-6

kernel_name = 'hybrid_conv_rglru_mla_interleaved'


def rms_norm(x, g):
    xf = x.astype(jnp.float32)
    y = xf * lax.rsqrt(jnp.mean(xf * xf, axis=-1, keepdims=True) + NORM_EPS)
    return (y * g.astype(jnp.float32)).astype(x.dtype)


def causal_depthwise_conv(x, w):
    width, ch = w.shape
    return lax.conv_general_dilated(
        x, w[:, None, :].astype(x.dtype), window_strides=(1,),
        padding=[(width - 1, 0)], dimension_numbers=('NWC', 'WIO', 'NWC'),
        feature_group_count=ch)


def short_conv_mixer(xn, w_in, w_conv, w_out):
    b_gate, c_gate, h = jnp.split(xn @ w_in, 3, axis=-1)
    y = b_gate * causal_depthwise_conv(c_gate * h, w_conv)
    return y @ w_out


def rg_lru(xs, wa, ba, wx, bx, lam):
    bsz, seq, width = xs.shape
    xb = xs.reshape(bsz, seq, LRU_BLOCKS, LRU_BLOCK_W)
    r = jax.nn.sigmoid(jnp.einsum('bsnd,nde->bsne', xb, wa) + ba).reshape(bsz, seq, width)
    i = jax.nn.sigmoid(jnp.einsum('bsnd,nde->bsne', xb, wx) + bx).reshape(bsz, seq, width)
    log_a = LRU_C * r.astype(jnp.float32) * jax.nn.log_sigmoid(lam.astype(jnp.float32))
    a = jnp.exp(log_a)
    mult = jnp.sqrt(-jnp.expm1(2.0 * log_a))
    b = mult * (i * xs).astype(jnp.float32)

    def combine(left, right):
        a_l, b_l = left
        a_r, b_r = right
        return a_l * a_r, a_r * b_l + b_r

    _, h = lax.associative_scan(combine, (a, b), axis=1)
    return h.astype(xs.dtype)


def recurrent_mixer(xn, w_in, conv_w, conv_b, gate_a_w, gate_a_b, gate_x_w, gate_x_b, lam, w_out):
    gate_branch, rec_branch = jnp.split(xn @ w_in, 2, axis=-1)
    gate = jax.nn.gelu(gate_branch, approximate=True)
    rec = causal_depthwise_conv(rec_branch, conv_w) + conv_b
    h = rg_lru(rec, gate_a_w, gate_a_b, gate_x_w, gate_x_b, lam)
    return (gate * h) @ w_out


def apply_rope(t, cos, sin):
    half = t.shape[-1] // 2
    tf = t.astype(jnp.float32)
    t1, t2 = tf[..., :half], tf[..., half:]
    return jnp.concatenate([t1 * cos - t2 * sin, t2 * cos + t1 * sin], axis=-1).astype(t.dtype)


def mla_mixer(xn, cos, sin, w_down, q_norm, kv_norm, w_uq, w_ukv,
              qn_norm, qr_norm, kn_norm, kr_norm, w_o):
    bsz, seq, _ = xn.shape
    c = xn @ w_down
    c_q = c[..., :Q_LORA_RANK]
    c_kv = c[..., Q_LORA_RANK:Q_LORA_RANK + KV_LORA_RANK]
    k_rope = c[..., Q_LORA_RANK + KV_LORA_RANK:]

    q = (rms_norm(c_q, q_norm) @ w_uq).reshape(bsz, seq, MLA_HEADS, QK_NOPE_DIM + QK_ROPE_DIM)
    kv = (rms_norm(c_kv, kv_norm) @ w_ukv).reshape(bsz, seq, MLA_HEADS, QK_NOPE_DIM + V_HEAD_DIM)
    q_nope = rms_norm(q[..., :QK_NOPE_DIM], qn_norm)
    q_rope = apply_rope(rms_norm(q[..., QK_NOPE_DIM:], qr_norm), cos[:, :, None, :], sin[:, :, None, :])
    k_nope = rms_norm(kv[..., :QK_NOPE_DIM], kn_norm)
    v = kv[..., QK_NOPE_DIM:]
    k_rope = apply_rope(rms_norm(k_rope, kr_norm), cos, sin)

    scale = 1.0 / math.sqrt(QK_NOPE_DIM + QK_ROPE_DIM)
    n_blk = seq // Q_BLOCK

    def to_blocks(t):
        return t.reshape(bsz, n_blk, Q_BLOCK, *t.shape[2:]).transpose(1, 0, 2, 3, 4)

    k_idx = jnp.arange(seq)

    def attend(args):
        qn_b, qr_b, start = args
        s = (jnp.einsum('bqhd,bkhd->bhqk', qn_b, k_nope)
             + jnp.einsum('bqhr,bkr->bhqk', qr_b, k_rope)).astype(jnp.float32) * scale
        q_idx = start + jnp.arange(Q_BLOCK)
        mask = k_idx[None, :] <= q_idx[:, None]
        s = jnp.where(mask, s, jnp.finfo(jnp.float32).min)
        p = jax.nn.softmax(s, axis=-1).astype(v.dtype)
        return jnp.einsum('bhqk,bkhd->bqhd', p, v)

    starts = jnp.arange(n_blk, dtype=jnp.int32) * Q_BLOCK
    o = lax.map(attend, (to_blocks(q_nope), to_blocks(q_rope), starts))
    o = o.transpose(1, 0, 2, 3, 4).reshape(bsz, seq, MLA_HEADS * V_HEAD_DIM)
    return o @ w_o


def swiglu_ffn(xn, w_gu, w_down):
    gate, up = jnp.split(xn @ w_gu, 2, axis=-1)
    return (jax.nn.silu(gate) * up) @ w_down


def setup_inputs(seed: int = 0) -> dict:
    key = jax.random.key(seed)
    ks = iter(jax.random.split(key, 40))
    f32 = jnp.float32
    res = (2 * DEPTH) ** -0.5

    def nrm(shape, scale):
        return jax.random.normal(next(ks), shape, f32) * scale

    def gain(shape):
        return 1.0 + 0.05 * jax.random.normal(next(ks), shape, f32)

    x = jax.random.normal(next(ks), (BATCH, SEQ, D_MODEL), f32)
    offset = jax.random.randint(next(ks), (BATCH, 1), 0, 1024, dtype=jnp.int32)
    positions = offset + jnp.arange(SEQ, dtype=jnp.int32)[None, :]

    mix_norm = gain((DEPTH, D_MODEL))

    nA = N_CONV_LAYERS
    conv_w_in = nrm((nA, D_MODEL, 3 * D_MODEL), D_MODEL ** -0.5)
    conv_w = nrm((nA, CONV_WIDTH, D_MODEL), CONV_WIDTH ** -0.5)
    conv_w_out = nrm((nA, D_MODEL, D_MODEL), D_MODEL ** -0.5 * res)

    nB = N_LRU_LAYERS
    lru_w_in = nrm((nB, D_MODEL, 2 * LRU_WIDTH), D_MODEL ** -0.5)
    lru_conv_w = nrm((nB, LRU_CONV_WIDTH, LRU_WIDTH), LRU_CONV_WIDTH ** -0.5)
    lru_conv_b = nrm((nB, LRU_WIDTH), 0.01)
    lru_gate_a_w = nrm((nB, LRU_BLOCKS, LRU_BLOCK_W, LRU_BLOCK_W), LRU_BLOCK_W ** -0.5)
    lru_gate_a_b = nrm((nB, LRU_BLOCKS, LRU_BLOCK_W), 0.01)
    lru_gate_x_w = nrm((nB, LRU_BLOCKS, LRU_BLOCK_W, LRU_BLOCK_W), LRU_BLOCK_W ** -0.5)
    lru_gate_x_b = nrm((nB, LRU_BLOCKS, LRU_BLOCK_W), 0.01)
    a0 = jax.random.uniform(next(ks), (nB, LRU_WIDTH), f32, 0.9, 0.999)
    lru_lambda = jnp.log(a0) - jnp.log1p(-a0)
    lru_w_out = nrm((nB, LRU_WIDTH, D_MODEL), LRU_WIDTH ** -0.5 * res)

    nC = N_MLA_LAYERS
    mla_w_down = nrm((nC, D_MODEL, Q_LORA_RANK + KV_LORA_RANK + QK_ROPE_DIM), D_MODEL ** -0.5)
    mla_q_norm = gain((nC, Q_LORA_RANK))
    mla_kv_norm = gain((nC, KV_LORA_RANK))
    mla_w_uq = nrm((nC, Q_LORA_RANK, MLA_HEADS * (QK_NOPE_DIM + QK_ROPE_DIM)), Q_LORA_RANK ** -0.5)
    mla_w_ukv = nrm((nC, KV_LORA_RANK, MLA_HEADS * (QK_NOPE_DIM + V_HEAD_DIM)), KV_LORA_RANK ** -0.5)
    mla_qn_norm = gain((nC, QK_NOPE_DIM))
    mla_qr_norm = gain((nC, QK_ROPE_DIM))
    mla_kn_norm = gain((nC, QK_NOPE_DIM))
    mla_kr_norm = gain((nC, QK_ROPE_DIM))
    mla_w_o = nrm((nC, MLA_HEADS * V_HEAD_DIM, D_MODEL), (MLA_HEADS * V_HEAD_DIM) ** -0.5 * res)

    ffn_norm = gain((DEPTH, D_MODEL))
    ffn_w_gu = nrm((DEPTH, D_MODEL, 2 * D_FF), D_MODEL ** -0.5)
    ffn_w_down = nrm((DEPTH, D_FF, D_MODEL), D_FF ** -0.5 * res)

    return {'x': x, 'positions': positions, 'mix_norm': mix_norm,
            'conv_w_in': conv_w_in, 'conv_w': conv_w, 'conv_w_out': conv_w_out,
            'lru_w_in': lru_w_in, 'lru_conv_w': lru_conv_w, 'lru_conv_b': lru_conv_b,
            'lru_gate_a_w': lru_gate_a_w, 'lru_gate_a_b': lru_gate_a_b,
            'lru_gate_x_w': lru_gate_x_w, 'lru_gate_x_b': lru_gate_x_b,
            'lru_lambda': lru_lambda, 'lru_w_out': lru_w_out,
            'mla_w_down': mla_w_down, 'mla_q_norm': mla_q_norm, 'mla_kv_norm': mla_kv_norm,
            'mla_w_uq': mla_w_uq, 'mla_w_ukv': mla_w_ukv,
            'mla_qn_norm': mla_qn_norm, 'mla_qr_norm': mla_qr_norm,
            'mla_kn_norm': mla_kn_norm, 'mla_kr_norm': mla_kr_norm, 'mla_w_o': mla_w_o,
            'ffn_norm': ffn_norm, 'ffn_w_gu': ffn_w_gu, 'ffn_w_down': ffn_w_down}


def reference(x, positions, mix_norm, conv_w_in, conv_w, conv_w_out,
              lru_w_in, lru_conv_w, lru_conv_b, lru_gate_a_w, lru_gate_a_b,
              lru_gate_x_w, lru_gate_x_b, lru_lambda, lru_w_out,
              mla_w_down, mla_q_norm, mla_kv_norm, mla_w_uq, mla_w_ukv,
              mla_qn_norm, mla_qr_norm, mla_kn_norm, mla_kr_norm, mla_w_o,
              ffn_norm, ffn_w_gu, ffn_w_down):
    inv_freq = ROPE_THETA ** (-jnp.arange(0, QK_ROPE_DIM, 2, dtype=jnp.float32) / QK_ROPE_DIM)
    angle = positions.astype(jnp.float32)[..., None] * inv_freq
    cos, sin = jnp.cos(angle), jnp.sin(angle)

    h = x
    for i in range(DEPTH):
        kind, j = i % N_MIXERS, i // N_MIXERS
        xn = rms_norm(h, mix_norm[i])
        if kind == 0:
            y = short_conv_mixer(xn, conv_w_in[j], conv_w[j], conv_w_out[j])
        elif kind == 1:
            y = recurrent_mixer(xn, lru_w_in[j], lru_conv_w[j], lru_conv_b[j],
                                lru_gate_a_w[j], lru_gate_a_b[j], lru_gate_x_w[j],
                                lru_gate_x_b[j], lru_lambda[j], lru_w_out[j])
        else:
            y = mla_mixer(xn, cos, sin, mla_w_down[j], mla_q_norm[j], mla_kv_norm[j],
                          mla_w_uq[j], mla_w_ukv[j], mla_qn_norm[j], mla_qr_norm[j],
                          mla_kn_norm[j], mla_kr_norm[j], mla_w_o[j])
        h = h + y
        h = h + swiglu_ffn(rms_norm(h, ffn_norm[i]), ffn_w_gu[i], ffn_w_down[i])
    return h
```

```python
import functools
import math

import jax
import jax.numpy as jnp
from jax import lax
from jax.experimental import pallas as pl
from jax.experimental.pallas import tpu as pltpu

F32 = jnp.float32
BF16 = jnp.bfloat16

D_MODEL = 1024
DEPTH = 4
N_MIXERS = 3
CONV_WIDTH = 3
LRU_WIDTH = 1280
LRU_BLOCKS = 10
LRU_BLOCK_W = LRU_WIDTH // LRU_BLOCKS
LRU_CONV_WIDTH = 4
LRU_C = 8.0
MLA_HEADS = 8
Q_LORA_RANK = 384
KV_LORA_RANK = 256
QK_NOPE_DIM = 128
QK_ROPE_DIM = 64
V_HEAD_DIM = 128
ROPE_THETA = 10000.0
D_FF = 2816
NORM_EPS = 1e-6

LANES = 128
SUBLANES = 8
QK_DIM_PADDED = 2 * LANES
VMEM_LIMIT = 56 * 1024 * 1024

TS_CONV = 512
TS_LRU = 256
TS_MLA = 512
TQ_ATTN = 512
TK_ATTN = 512
TM_FFN = 512
FFN_CHUNKS = 2


def _rms(x, g):
    return x * lax.rsqrt(jnp.mean(x * x, axis=-1, keepdims=True) + NORM_EPS) * g


def _dot(a, b):
    return jnp.dot(a, b, preferred_element_type=F32)


def _const_spec(shape):
    nd = len(shape)
    return pl.BlockSpec(shape, lambda *_: (0,) * nd, pipeline_mode=pl.Buffered(1))


def _shift_rows_prev(x, prev8, d):
    r = pltpu.roll(x, d, 0)
    hd = pltpu.roll(prev8, d, 0)
    rows8 = lax.broadcasted_iota(jnp.int32, prev8.shape, 0)
    first = jnp.where(rows8 < d, hd, r[:SUBLANES])
    return jnp.concatenate([first, r[SUBLANES:]], axis=0)


def _shift_rows_fill(x, d, fill):
    n, c = x.shape
    if d >= SUBLANES:
        return jnp.concatenate([jnp.full((d, c), fill, x.dtype), x[:n - d]], axis=0)
    r = pltpu.roll(x, d, 0)
    rows8 = lax.broadcasted_iota(jnp.int32, (SUBLANES, c), 0)
    first = jnp.where(rows8 < d, jnp.asarray(fill, x.dtype), r[:SUBLANES])
    return jnp.concatenate([first, r[SUBLANES:]], axis=0)


def _conv_layer_kernel(h_ref, g_ref, win_ref, cw_ref, wout_ref, o_ref, carry_ref):
    @pl.when(pl.program_id(1) == 0)
    def _():
        carry_ref[...] = jnp.zeros_like(carry_ref)

    h = h_ref[0]
    xn = _rms(h, g_ref[...]).astype(BF16)
    bch = _dot(xn, win_ref[...])
    b_gate = bch[:, :D_MODEL]
    u = bch[:, D_MODEL:2 * D_MODEL] * bch[:, 2 * D_MODEL:]
    prev8 = carry_ref[...]
    cw = cw_ref[...]
    conv = (cw[0:1] * _shift_rows_prev(u, prev8, 2)
            + cw[1:2] * _shift_rows_prev(u, prev8, 1)
            + cw[2:3] * u)
    carry_ref[...] = u[u.shape[0] - SUBLANES:]
    y = _dot((b_gate * conv).astype(BF16), wout_ref[...])
    o_ref[0] = h + y


def _conv_layer(h, g, w_in, cw, w_out):
    bsz, seq, d = h.shape
    ts = TS_CONV
    return pl.pallas_call(
        _conv_layer_kernel,
        out_shape=jax.ShapeDtypeStruct(h.shape, h.dtype),
        grid=(bsz, seq // ts),
        in_specs=[
            pl.BlockSpec((1, ts, d), lambda b, s: (b, s, 0)),
            _const_spec(g.shape), _const_spec(w_in.shape), _const_spec(cw.shape),
            _const_spec(w_out.shape),
        ],
        out_specs=pl.BlockSpec((1, ts, d), lambda b, s: (b, s, 0)),
        scratch_shapes=[pltpu.VMEM((SUBLANES, d), F32)],
        compiler_params=pltpu.CompilerParams(
            dimension_semantics=("arbitrary", "arbitrary"), vmem_limit_bytes=VMEM_LIMIT),
        name="conv_layer",
    )(h, g, w_in, cw, w_out)


def _lru_layer_kernel(h_ref, g_ref, win_ref, cw_ref, cb_ref, wax_ref, ba_ref, bx_ref, lam_ref,
                      wout_ref, o_ref, xcarry_ref, hcarry_ref):
    @pl.when(pl.program_id(1) == 0)
    def _():
        xcarry_ref[...] = jnp.zeros_like(xcarry_ref)
        hcarry_ref[...] = jnp.zeros_like(hcarry_ref)

    h = h_ref[0]
    ts = h.shape[0]
    xn = _rms(h, g_ref[...]).astype(BF16)
    gr = _dot(xn, win_ref[...])
    gate = jax.nn.gelu(gr[:, :LRU_WIDTH], approximate=True)
    xr = gr[:, LRU_WIDTH:]
    prev8 = xcarry_ref[...]
    cw = cw_ref[...]
    rec = (cw[0:1] * _shift_rows_prev(xr, prev8, 3)
           + cw[1:2] * _shift_rows_prev(xr, prev8, 2)
           + cw[2:3] * _shift_rows_prev(xr, prev8, 1)
           + cw[3:4] * xr) + cb_ref[...]
    xcarry_ref[...] = xr[ts - SUBLANES:]

    lam = lam_ref[...]
    log_sig_lam = jnp.minimum(lam, 0.0) - jnp.log1p(jnp.exp(-jnp.abs(lam)))
    ba = ba_ref[...]
    bx = bx_ref[...]
    h_prev = hcarry_ref[...][SUBLANES - 1:SUBLANES]
    outs = []
    for n in range(LRU_BLOCKS):
        sl = slice(n * LRU_BLOCK_W, (n + 1) * LRU_BLOCK_W)
        xb = rec[:, sl]
        ri = _dot(xb.astype(BF16), wax_ref[n])
        r = jax.nn.sigmoid(ri[:, :LRU_BLOCK_W] + ba[:, sl])
        i = jax.nn.sigmoid(ri[:, LRU_BLOCK_W:] + bx[:, sl])
        log_a = LRU_C * r * log_sig_lam[:, sl]
        a = jnp.exp(log_a)
        mult = jnp.sqrt(-jnp.tanh(log_a) * (a * a + 1.0))
        b = mult * (i * xb)
        d = 1
        while d < ts:
            a_sh = _shift_rows_fill(a, d, 1.0)
            b_sh = _shift_rows_fill(b, d, 0.0)
            b = a * b_sh + b
            a = a * a_sh
            d *= 2
        outs.append(b + a * h_prev[:, sl])
    hs = jnp.concatenate(outs, axis=1)
    hcarry_ref[...] = hs[ts - SUBLANES:]
    y = _dot((gate * hs).astype(BF16), wout_ref[...])
    o_ref[0] = h + y


def _lru_layer(h, g, w_in, cw, cb, wax, ba, bx, lam, w_out):
    bsz, seq, d = h.shape
    ts = TS_LRU
    consts = (g, w_in, cw, cb, wax, ba, bx, lam, w_out)
    return pl.pallas_call(
        _lru_layer_kernel,
        out_shape=jax.ShapeDtypeStruct(h.shape, h.dtype),
        grid=(bsz, seq // ts),
        in_specs=[pl.BlockSpec((1, ts, d), lambda b, s: (b, s, 0))]
        + [_const_spec(c.shape) for c in consts],
        out_specs=pl.BlockSpec((1, ts, d), lambda b, s: (b, s, 0)),
        scratch_shapes=[pltpu.VMEM((SUBLANES, LRU_WIDTH), F32),
                        pltpu.VMEM((SUBLANES, LRU_WIDTH), F32)],
        compiler_params=pltpu.CompilerParams(
            dimension_semantics=("arbitrary", "arbitrary"), vmem_limit_bytes=VMEM_LIMIT),
        name="lru_layer",
    )(h, *consts)


def _mla_proj_kernel(h_ref, pos_ref, g_ref, wd_ref, qg_ref, kvg_ref, wuq_ref, wukv_ref,
                     qng_ref, qrg_ref, kng_ref, krg_ref, freq_ref, sign_ref,
                     q_ref, k_ref, v_ref):
    h = h_ref[0]
    xn = _rms(h, g_ref[...]).astype(BF16)
    c = _dot(xn, wd_ref[...])
    c_q = c[:, :Q_LORA_RANK]
    c_kv = c[:, Q_LORA_RANK:Q_LORA_RANK + KV_LORA_RANK]
    k_rope = c[:, Q_LORA_RANK + KV_LORA_RANK:]
    q = _dot(_rms(c_q, qg_ref[...]).astype(BF16), wuq_ref[...])
    kv = _dot(_rms(c_kv, kvg_ref[...]).astype(BF16), wukv_ref[...])

    ang = pos_ref[0] * freq_ref[...]
    cos = jnp.cos(ang)
    sin = jnp.sin(ang) * sign_ref[...]
    half = QK_ROPE_DIM // 2

    def rope_slab(x, gain):
        ms = jnp.sum(x * x, axis=-1, keepdims=True) * (1.0 / QK_ROPE_DIM)
        y = x * lax.rsqrt(ms + NORM_EPS) * gain
        swapped = pltpu.roll(y, half, 1) + pltpu.roll(y, LANES - half, 1)
        return y * cos + swapped * sin

    scale = 1.0 / math.sqrt(QK_NOPE_DIM + QK_ROPE_DIM)
    k_rope = rope_slab(k_rope, krg_ref[...]).astype(BF16)
    nope_all = MLA_HEADS * QK_NOPE_DIM
    for hh in range(MLA_HEADS):
        sl = slice(hh * LANES, (hh + 1) * LANES)
        sl2 = slice(nope_all + hh * LANES, nope_all + (hh + 1) * LANES)
        q_nope = _rms(q[:, sl], qng_ref[...])
        q_rope = rope_slab(q[:, sl2], qrg_ref[...])
        q_ref[0, hh, :, :LANES] = (q_nope * scale).astype(BF16)
        q_ref[0, hh, :, LANES:] = (q_rope * scale).astype(BF16)
        k_ref[0, hh, :, :LANES] = _rms(kv[:, sl], kng_ref[...]).astype(BF16)
        k_ref[0, hh, :, LANES:] = k_rope
        v_ref[0, hh] = kv[:, sl2].astype(BF16)


def _mla_proj(h, posb, consts):
    bsz, seq, d = h.shape
    ts = TS_MLA
    qk_shape = jax.ShapeDtypeStruct((bsz, MLA_HEADS, seq, QK_DIM_PADDED), BF16)
    v_shape = jax.ShapeDtypeStruct((bsz, MLA_HEADS, seq, V_HEAD_DIM), BF16)
    return pl.pallas_call(
        _mla_proj_kernel,
        out_shape=(qk_shape, qk_shape, v_shape),
        grid=(bsz, seq // ts),
        in_specs=[pl.BlockSpec((1, ts, d), lambda b, s: (b, s, 0)),
                  pl.BlockSpec((1, ts, LANES), lambda b, s: (b, s, 0))]
        + [_const_spec(c.shape) for c in consts],
        out_specs=(pl.BlockSpec((1, MLA_HEADS, ts, QK_DIM_PADDED), lambda b, s: (b, 0, s, 0)),
                   pl.BlockSpec((1, MLA_HEADS, ts, QK_DIM_PADDED), lambda b, s: (b, 0, s, 0)),
                   pl.BlockSpec((1, MLA_HEADS, ts, V_HEAD_DIM), lambda b, s: (b, 0, s, 0))),
        compiler_params=pltpu.CompilerParams(
            dimension_semantics=("arbitrary", "arbitrary"), vmem_limit_bytes=VMEM_LIMIT),
        name="mla_proj",
    )(h, posb, *consts)


def _attn_kernel(q_ref, k_ref, v_ref, o_ref, m_ref, l_ref, acc_ref):
    qi = pl.program_id(2)
    tq = q_ref.shape[2]
    tk = TK_ATTN
    q = q_ref[0, 0]
    m_ref[...] = jnp.full_like(m_ref, -jnp.inf)
    l_ref[...] = jnp.zeros_like(l_ref)
    acc_ref[...] = jnp.zeros_like(acc_ref)

    def block(kb, masked):
        start = pl.multiple_of(kb * tk, tk)
        k = k_ref[0, 0, pl.ds(start, tk), :]
        v = v_ref[0, 0, pl.ds(start, tk), :]
        s = lax.dot_general(q, k, (((1,), (1,)), ((), ())), preferred_element_type=F32)
        if masked:
            row = lax.broadcasted_iota(jnp.int32, s.shape, 0)
            col = lax.broadcasted_iota(jnp.int32, s.shape, 1)
            s = jnp.where(col <= row, s, jnp.finfo(F32).min)
        m_prev = m_ref[...]
        m_next = jnp.maximum(m_prev, jnp.max(s, axis=1, keepdims=True))
        alpha = jnp.exp(m_prev - m_next)
        p = jnp.exp(s - m_next)
        l_ref[...] = alpha * l_ref[...] + jnp.sum(p, axis=1, keepdims=True)
        acc_ref[...] = alpha * acc_ref[...] + _dot(p.astype(BF16), v)
        m_ref[...] = m_next

    def body(kb, carry):
        block(kb, False)
        return carry

    lax.fori_loop(0, qi * (tq // tk), body, 0)
    block(qi, True)
    o_ref[0] = (acc_ref[...] / l_ref[...]).astype(o_ref.dtype)


def _attention(q, k, v):
    bsz, nh, seq, dq = q.shape
    dv = v.shape[-1]
    tq = TQ_ATTN
    assert TQ_ATTN == TK_ATTN
    return pl.pallas_call(
        _attn_kernel,
        out_shape=jax.ShapeDtypeStruct((bsz, seq, nh * dv), BF16),
        grid=(bsz, nh, seq // tq),
        in_specs=[pl.BlockSpec((1, 1, tq, dq), lambda b, h, i: (b, h, i, 0)),
                  pl.BlockSpec((1, 1, seq, dq), lambda b, h, i: (b, h, 0, 0)),
                  pl.BlockSpec((1, 1, seq, dv), lambda b, h, i: (b, h, 0, 0))],
        out_specs=pl.BlockSpec((1, tq, dv), lambda b, h, i: (b, i, h)),
        scratch_shapes=[pltpu.VMEM((tq, 1), F32), pltpu.VMEM((tq, 1), F32),
                        pltpu.VMEM((tq, dv), F32)],
        compiler_params=pltpu.CompilerParams(
            dimension_semantics=("arbitrary", "arbitrary", "arbitrary"),
            vmem_limit_bytes=VMEM_LIMIT),
        name="mla_attention",
    )(q, k, v)


def _ffn_body(h, g_ref, wgu_ref, wdn_ref, o_ref):
    xn = _rms(h, g_ref[...]).astype(BF16)
    cw = D_FF // FFN_CHUNKS
    out = h
    for c in range(FFN_CHUNKS):
        gate = _dot(xn, wgu_ref[:, c * cw:(c + 1) * cw])
        up = _dot(xn, wgu_ref[:, D_FF + c * cw:D_FF + (c + 1) * cw])
        act = (jax.nn.silu(gate) * up).astype(BF16)
        out = out + _dot(act, wdn_ref[c * cw:(c + 1) * cw, :])
    o_ref[...] = out


def _ffn_kernel(h_ref, g_ref, wgu_ref, wdn_ref, o_ref):
    _ffn_body(h_ref[...], g_ref, wgu_ref, wdn_ref, o_ref)


def _proj_ffn_kernel(h_ref, a_ref, wo_ref, g_ref, wgu_ref, wdn_ref, o_ref):
    h = h_ref[...] + _dot(a_ref[...], wo_ref[...])
    _ffn_body(h, g_ref, wgu_ref, wdn_ref, o_ref)


def _ffn(h2, g, w_gu, w_dn, attn=None, w_o=None):
    t, d = h2.shape
    tm = TM_FFN
    tok = pl.BlockSpec((tm, d), lambda i: (i, 0))
    if attn is None:
        kern, args, specs = _ffn_kernel, (h2,), [tok]
    else:
        kern, args = _proj_ffn_kernel, (h2, attn, w_o)
        specs = [tok, pl.BlockSpec((tm, attn.shape[1]), lambda i: (i, 0)), _const_spec(w_o.shape)]
    return pl.pallas_call(
        kern,
        out_shape=jax.ShapeDtypeStruct(h2.shape, h2.dtype),
        grid=(t // tm,),
        in_specs=specs + [_const_spec(g.shape), _const_spec(w_gu.shape), _const_spec(w_dn.shape)],
        out_specs=tok,
        compiler_params=pltpu.CompilerParams(
            dimension_semantics=("arbitrary",), vmem_limit_bytes=VMEM_LIMIT),
        name="ffn" if attn is None else "proj_ffn",
    )(*args, g, w_gu, w_dn)


def _row(v):
    return v.reshape(1, -1)


def _pad_lanes(v, n):
    return jnp.pad(v, [(0, 0)] * (v.ndim - 1) + [(0, n - v.shape[-1])])


def kernel(x, positions, mix_norm, conv_w_in, conv_w, conv_w_out, lru_w_in, lru_conv_w, lru_conv_b, lru_gate_a_w, lru_gate_a_b, lru_gate_x_w, lru_gate_x_b, lru_lambda, lru_w_out, mla_w_down, mla_q_norm, mla_kv_norm, mla_w_uq, mla_w_ukv, mla_qn_norm, mla_qr_norm, mla_kn_norm, mla_kr_norm, mla_w_o, ffn_norm, ffn_w_gu, ffn_w_down):
    bsz, seq, d = x.shape
    nh = MLA_HEADS
    half = QK_ROPE_DIM // 2

    inv_freq = ROPE_THETA ** (-jnp.arange(0, QK_ROPE_DIM, 2, dtype=F32) / QK_ROPE_DIM)
    freq_slab = _row(_pad_lanes(jnp.concatenate([inv_freq, inv_freq]), LANES))
    sign_slab = _row(_pad_lanes(jnp.concatenate([-jnp.ones(half, F32), jnp.ones(half, F32)]), LANES))
    posb = jnp.broadcast_to(positions.astype(F32)[..., None], (bsz, seq, LANES))

    h = x
    for i in range(DEPTH):
        kind, j = i % N_MIXERS, i // N_MIXERS
        g = _row(mix_norm[i])
        attn = w_o = None
        if kind == 0:
            h = _conv_layer(h, g, conv_w_in[j].astype(BF16), conv_w[j], conv_w_out[j].astype(BF16))
        elif kind == 1:
            wax = jnp.concatenate([lru_gate_a_w[j], lru_gate_x_w[j]], axis=-1).astype(BF16)
            h = _lru_layer(h, g, lru_w_in[j].astype(BF16), lru_conv_w[j], _row(lru_conv_b[j]), wax,
                           _row(lru_gate_a_b[j]), _row(lru_gate_x_b[j]), _row(lru_lambda[j]),
                           lru_w_out[j].astype(BF16))
        else:
            wd = _pad_lanes(mla_w_down[j], Q_LORA_RANK + KV_LORA_RANK + LANES).astype(BF16)
            wq = mla_w_uq[j].reshape(Q_LORA_RANK, nh, QK_NOPE_DIM + QK_ROPE_DIM)
            wuq = jnp.concatenate(
                [wq[:, :, :QK_NOPE_DIM].reshape(Q_LORA_RANK, nh * QK_NOPE_DIM),
                 _pad_lanes(wq[:, :, QK_NOPE_DIM:], LANES).reshape(Q_LORA_RANK, nh * LANES)],
                axis=1).astype(BF16)
            wkv = mla_w_ukv[j].reshape(KV_LORA_RANK, nh, QK_NOPE_DIM + V_HEAD_DIM)
            wukv = jnp.concatenate(
                [wkv[:, :, :QK_NOPE_DIM].reshape(KV_LORA_RANK, nh * QK_NOPE_DIM),
                 wkv[:, :, QK_NOPE_DIM:].reshape(KV_LORA_RANK, nh * V_HEAD_DIM)], axis=1).astype(BF16)
            consts = (g, wd, _row(mla_q_norm[j]), _row(mla_kv_norm[j]), wuq, wukv,
                      _row(mla_qn_norm[j]), _row(_pad_lanes(mla_qr_norm[j], LANES)),
                      _row(mla_kn_norm[j]), _row(_pad_lanes(mla_kr_norm[j], LANES)),
                      freq_slab, sign_slab)
            q, k, v = _mla_proj(h, posb, consts)
            attn = _attention(q, k, v).reshape(bsz * seq, nh * V_HEAD_DIM)
            w_o = mla_w_o[j].astype(BF16)
        h = _ffn(h.reshape(bsz * seq, d), _row(ffn_norm[i]), ffn_w_gu[i].astype(BF16),
                 ffn_w_down[i].astype(BF16), attn, w_o).reshape(bsz, seq, d)
    return h
```

```python
import math

import jax
import jax.numpy as jnp
from jax import lax
from jax.experimental import pallas as pl
from jax.experimental.pallas import tpu as pltpu

F32 = jnp.float32
BF16 = jnp.bfloat16

D_MODEL = 1024
DEPTH = 4
N_MIXERS = 3
CONV_WIDTH = 3
LRU_WIDTH = 1280
LRU_BLOCKS = 10
LRU_BLOCK_W = LRU_WIDTH // LRU_BLOCKS
LRU_CONV_WIDTH = 4
LRU_C = 8.0
MLA_HEADS = 8
Q_LORA_RANK = 384
KV_LORA_RANK = 256
QK_NOPE_DIM = 128
QK_ROPE_DIM = 64
V_HEAD_DIM = 128
ROPE_THETA = 10000.0
D_FF = 2816
NORM_EPS = 1e-6

LANES = 128
SUBLANES = 8
QK_DIM_PADDED = 2 * LANES
VMEM_LIMIT = 56 * 1024 * 1024

TS_CONV = 512
TT_LRU = 64
TS_MLA = 512
TQ_ATTN = 512
ATTN_HEADS_PER_STEP = 2
TM_FFN = 512
FFN_CHUNKS = 2
LOG2E = 1.4426950408889634


def _rms(x, g):
    return x * lax.rsqrt(jnp.mean(x * x, axis=-1, keepdims=True) + NORM_EPS) * g


def _dot(a, b):
    return jnp.dot(a, b, preferred_element_type=F32)


def _const_spec(shape):
    nd = len(shape)
    return pl.BlockSpec(shape, lambda *_: (0,) * nd, pipeline_mode=pl.Buffered(1))


def _shift_rows_prev(x, prev8, d):
    r = pltpu.roll(x, d, 0)
    hd = pltpu.roll(prev8, d, 0)
    rows8 = lax.broadcasted_iota(jnp.int32, prev8.shape, 0)
    first = jnp.where(rows8 < d, hd, r[:SUBLANES])
    return jnp.concatenate([first, r[SUBLANES:]], axis=0)


def _conv_layer_kernel(h_ref, g_ref, win_ref, cw_ref, wout_ref, o_ref, carry_ref):
    @pl.when(pl.program_id(1) == 0)
    def _():
        carry_ref[...] = jnp.zeros_like(carry_ref)

    h = h_ref[0]
    xn = _rms(h, g_ref[...]).astype(BF16)
    bch = _dot(xn, win_ref[...])
    b_gate = bch[:, :D_MODEL]
    u = bch[:, D_MODEL:2 * D_MODEL] * bch[:, 2 * D_MODEL:]
    prev8 = carry_ref[...]
    cw = cw_ref[...]
    conv = (cw[0:1] * _shift_rows_prev(u, prev8, 2)
            + cw[1:2] * _shift_rows_prev(u, prev8, 1)
            + cw[2:3] * u)
    carry_ref[...] = u[u.shape[0] - SUBLANES:]
    y = _dot((b_gate * conv).astype(BF16), wout_ref[...])
    o_ref[...] = (h + y).reshape(o_ref.shape)


def _conv_layer(h, g, w_in, cw, w_out, time_major_out):
    bsz, seq, d = h.shape
    ts = TS_CONV
    if time_major_out:
        out_shape = jax.ShapeDtypeStruct((seq, bsz * d), h.dtype)
        out_spec = pl.BlockSpec((ts, d), lambda b, s: (s, b))
    else:
        out_shape = jax.ShapeDtypeStruct(h.shape, h.dtype)
        out_spec = pl.BlockSpec((1, ts, d), lambda b, s: (b, s, 0))
    return pl.pallas_call(
        _conv_layer_kernel,
        out_shape=out_shape,
        grid=(bsz, seq // ts),
        in_specs=[
            pl.BlockSpec((1, ts, d), lambda b, s: (b, s, 0)),
            _const_spec(g.shape), _const_spec(w_in.shape), _const_spec(cw.shape),
            _const_spec(w_out.shape),
        ],
        out_specs=out_spec,
        scratch_shapes=[pltpu.VMEM((SUBLANES, d), F32)],
        compiler_params=pltpu.CompilerParams(
            dimension_semantics=("arbitrary", "arbitrary"), vmem_limit_bytes=VMEM_LIMIT),
        name="conv_layer",
    )(h, g, w_in, cw, w_out)


def _lru_layer_kernel(h_ref, g_ref, win_ref, cw_ref, cb_ref, wax_ref, ba_ref, bx_ref, lam_ref,
                      wout_ref, o_ref, xcarry_ref, hcarry_ref):
    @pl.when(pl.program_id(0) == 0)
    def _():
        xcarry_ref[...] = jnp.zeros_like(xcarry_ref)
        hcarry_ref[...] = jnp.zeros_like(hcarry_ref)

    tt, nb, d = h_ref.shape
    rows = tt * nb
    h = h_ref[...].reshape(rows, d)
    xn = _rms(h, g_ref[...]).astype(BF16)
    gr = _dot(xn, win_ref[...])
    gate = jax.nn.gelu(gr[:, :LRU_WIDTH], approximate=True)
    xr = gr[:, LRU_WIDTH:]
    halo = (LRU_CONV_WIDTH - 1) * nb
    ext = jnp.concatenate([xcarry_ref[...], xr], axis=0)
    xcarry_ref[...] = xr[rows - halo:]
    cw = cw_ref[...]
    rec = cb_ref[...] + cw[LRU_CONV_WIDTH - 1:LRU_CONV_WIDTH] * xr
    for k in range(LRU_CONV_WIDTH - 1):
        rec = rec + cw[k:k + 1] * ext[k * nb:k * nb + rows]

    lam = lam_ref[...]
    log_sig_lam = jnp.minimum(lam, 0.0) - jnp.log1p(jnp.exp(-jnp.abs(lam)))
    ba = ba_ref[...]
    bx = bx_ref[...]
    h_last = hcarry_ref[...]
    outs = []
    for n in range(LRU_BLOCKS):
        sl = slice(n * LRU_BLOCK_W, (n + 1) * LRU_BLOCK_W)
        xb = rec[:, sl]
        ri = _dot(xb.astype(BF16), wax_ref[n])
        r = jax.nn.sigmoid(ri[:, :LRU_BLOCK_W] + ba[:, sl])
        i = jax.nn.sigmoid(ri[:, LRU_BLOCK_W:] + bx[:, sl])
        log_a = LRU_C * r * log_sig_lam[:, sl]
        a = jnp.exp(log_a)
        z = -jnp.tanh(log_a) * (a * a + 1.0)
        mult = jnp.where(z > 0.0, z * lax.rsqrt(z), 0.0)
        b = mult * (i * xb)
        hp = h_last[:, sl]
        steps = []
        for t in range(tt):
            hp = a[t * nb:(t + 1) * nb] * hp + b[t * nb:(t + 1) * nb]
            steps.append(hp)
        outs.append(jnp.concatenate(steps, axis=0))
    hs = jnp.concatenate(outs, axis=1)
    hcarry_ref[...] = hs[rows - nb:]
    y = _dot((gate * hs).astype(BF16), wout_ref[...])
    o_ref[...] = (h + y).reshape(tt, nb, d)


def _lru_layer(h, g, w_in, cw, cb, wax, ba, bx, lam, w_out):
    seq, bsz, d = h.shape
    assert bsz == SUBLANES
    tt = TT_LRU
    consts = (g, w_in, cw, cb, wax, ba, bx, lam, w_out)
    tok = pl.BlockSpec((tt, bsz, d), lambda s: (s, 0, 0))
    return pl.pallas_call(
        _lru_layer_kernel,
        out_shape=jax.ShapeDtypeStruct(h.shape, h.dtype),
        grid=(seq // tt,),
        in_specs=[tok] + [_const_spec(c.shape) for c in consts],
        out_specs=tok,
        scratch_shapes=[pltpu.VMEM(((LRU_CONV_WIDTH - 1) * bsz, LRU_WIDTH), F32),
                        pltpu.VMEM((bsz, LRU_WIDTH), F32)],
        compiler_params=pltpu.CompilerParams(
            dimension_semantics=("arbitrary",), vmem_limit_bytes=VMEM_LIMIT),
        name="lru_layer",
    )(h, *consts)


def _mla_proj_kernel(h_ref, pos_ref, g_ref, wd_ref, qg_ref, kvg_ref, wuq_ref, wukv_ref,
                     qng_ref, qrg_ref, kng_ref, krg_ref, freq_ref, sign_ref, live_ref,
                     q_ref, k_ref, v_ref):
    h = h_ref[...]
    xn = _rms(h, g_ref[...]).astype(BF16)
    c = _dot(xn, wd_ref[...])
    c_q = c[:, :Q_LORA_RANK]
    c_kv = c[:, Q_LORA_RANK:Q_LORA_RANK + KV_LORA_RANK]
    k_rope = c[:, Q_LORA_RANK + KV_LORA_RANK:]
    q = _dot(_rms(c_q, qg_ref[...]).astype(BF16), wuq_ref[...])
    kv = _dot(_rms(c_kv, kvg_ref[...]).astype(BF16), wukv_ref[...])

    ang = pos_ref[0] * freq_ref[...]
    cos = jnp.cos(ang) * live_ref[...]
    sin = jnp.sin(ang) * sign_ref[...]
    half = QK_ROPE_DIM // 2

    def rope_slab(x, gain):
        ms = jnp.sum(x * x, axis=-1, keepdims=True) * (1.0 / (2 * QK_ROPE_DIM))
        y = x * lax.rsqrt(ms + NORM_EPS) * gain
        return y * cos + pltpu.roll(y, half, 1) * sin

    q_scale = LOG2E / math.sqrt(QK_NOPE_DIM + QK_ROPE_DIM)
    k_rope = rope_slab(k_rope, krg_ref[...]).astype(BF16)
    ones = jnp.ones((h.shape[0], LANES), BF16)
    nope_all = MLA_HEADS * QK_NOPE_DIM
    for hh in range(MLA_HEADS):
        sl = slice(hh * LANES, (hh + 1) * LANES)
        sl2 = slice(nope_all + hh * LANES, nope_all + (hh + 1) * LANES)
        q_nope = _rms(q[:, sl], qng_ref[...])
        q_rope = rope_slab(q[:, sl2], qrg_ref[...])
        q_ref[0, hh, :, :LANES] = (q_nope * q_scale).astype(BF16)
        q_ref[0, hh, :, LANES:] = (q_rope * q_scale).astype(BF16)
        k_ref[0, hh, :, :LANES] = _rms(kv[:, sl], kng_ref[...]).astype(BF16)
        k_ref[0, hh, :, LANES:] = k_rope
        v_ref[0, hh, :, :LANES] = kv[:, sl2].astype(BF16)
        v_ref[0, hh, :, LANES:] = ones


def _mla_proj(h_tm, posb, consts):
    bsz, seq, _ = posb.shape
    d = h_tm.shape[1] // bsz
    ts = TS_MLA
    out_sds = jax.ShapeDtypeStruct((bsz, MLA_HEADS, seq, QK_DIM_PADDED), BF16)
    out_spec = pl.BlockSpec((1, MLA_HEADS, ts, QK_DIM_PADDED), lambda b, s: (b, 0, s, 0))
    return pl.pallas_call(
        _mla_proj_kernel,
        out_shape=(out_sds, out_sds, out_sds),
        grid=(bsz, seq // ts),
        in_specs=[pl.BlockSpec((ts, d), lambda b, s: (s, b)),
                  pl.BlockSpec((1, ts, LANES), lambda b, s: (b, s, 0))]
        + [_const_spec(c.shape) for c in consts],
        out_specs=(out_spec, out_spec, out_spec),
        compiler_params=pltpu.CompilerParams(
            dimension_semantics=("arbitrary", "arbitrary"), vmem_limit_bytes=VMEM_LIMIT),
        name="mla_proj",
    )(h_tm, posb, *consts)


def _attn_kernel(q_ref, k_ref, v_ref, o_ref, m_ref, acc_ref, s0_ref, s1_ref):
    qi = pl.program_id(2)
    nhp, tq = q_ref.shape[1], q_ref.shape[2]
    tk = tq
    m_ref[...] = jnp.full_like(m_ref, -jnp.inf)
    acc_ref[...] = jnp.zeros_like(acc_ref)

    def logits(kb, s_ref):
        start = pl.multiple_of(kb * tk, tk)
        for hp in range(nhp):
            k = k_ref[0, hp, pl.ds(start, tk), :]
            s_ref[hp] = lax.dot_general(q_ref[0, hp], k, (((1,), (1,)), ((), ())),
                                        preferred_element_type=F32)

    def softmax_pv(kb, s_ref, masked):
        start = pl.multiple_of(kb * tk, tk)
        for hp in range(nhp):
            s = s_ref[hp]
            if masked:
                row = lax.broadcasted_iota(jnp.int32, s.shape, 0)
                col = lax.broadcasted_iota(jnp.int32, s.shape, 1)
                s = jnp.where(col <= row, s, jnp.finfo(F32).min)
            m_prev = m_ref[hp]
            m_next = jnp.maximum(m_prev, jnp.max(s, axis=1, keepdims=True))
            alpha = jnp.exp2(m_prev - m_next)
            p = jnp.exp2(s - jnp.concatenate([m_next] * (tk // LANES), axis=1))
            v = v_ref[0, hp, pl.ds(start, tk), :]
            acc_ref[hp] = (jnp.concatenate([alpha, alpha], axis=1) * acc_ref[hp]
                           + _dot(p.astype(BF16), v))
            m_ref[hp] = m_next

    logits(0, s0_ref)

    def body(j, carry):
        kb = 2 * j
        logits(kb + 1, s1_ref)
        softmax_pv(kb, s0_ref, False)
        logits(kb + 2, s0_ref)
        softmax_pv(kb + 1, s1_ref, False)
        return carry

    lax.fori_loop(0, qi // 2, body, 0)

    @pl.when(qi % 2 == 0)
    def _():
        softmax_pv(qi, s0_ref, True)

    @pl.when(qi % 2 == 1)
    def _():
        logits(qi, s1_ref)
        softmax_pv(qi - 1, s0_ref, False)
        softmax_pv(qi, s1_ref, True)

    for hp in range(nhp):
        acc = acc_ref[hp]
        o_ref[0, :, hp * LANES:(hp + 1) * LANES] = (acc[:, :LANES] / acc[:, LANES:]).astype(o_ref.dtype)


def _attention(q, k, v1):
    bsz, nh, seq, dq = q.shape
    tq = TQ_ATTN
    hp = ATTN_HEADS_PER_STEP
    return pl.pallas_call(
        _attn_kernel,
        out_shape=jax.ShapeDtypeStruct((bsz, seq, nh * V_HEAD_DIM), BF16),
        grid=(bsz, nh // hp, seq // tq),
        in_specs=[pl.BlockSpec((1, hp, tq, dq), lambda b, h, i: (b, h, i, 0)),
                  pl.BlockSpec((1, hp, seq, dq), lambda b, h, i: (b, h, 0, 0)),
                  pl.BlockSpec((1, hp, seq, dq), lambda b, h, i: (b, h, 0, 0))],
        out_specs=pl.BlockSpec((1, tq, hp * V_HEAD_DIM), lambda b, h, i: (b, i, h)),
        scratch_shapes=[pltpu.VMEM((hp, tq, LANES), F32), pltpu.VMEM((hp, tq, dq), F32),
                        pltpu.VMEM((hp, tq, tq), F32), pltpu.VMEM((hp, tq, tq), F32)],
        compiler_params=pltpu.CompilerParams(
            dimension_semantics=("arbitrary", "arbitrary", "arbitrary"),
            vmem_limit_bytes=VMEM_LIMIT),
        name="mla_attention",
    )(q, k, v1)


def _ffn_body(h, g_ref, wgu_ref, wdn_ref, o_ref):
    xn = _rms(h, g_ref[...]).astype(BF16)
    cw = D_FF // FFN_CHUNKS
    out = h
    for c in range(FFN_CHUNKS):
        gate = _dot(xn, wgu_ref[:, c * cw:(c + 1) * cw])
        up = _dot(xn, wgu_ref[:, D_FF + c * cw:D_FF + (c + 1) * cw])
        act = (jax.nn.silu(gate) * up).astype(BF16)
        out = out + _dot(act, wdn_ref[c * cw:(c + 1) * cw, :])
    o_ref[...] = out


def _ffn_kernel(h_ref, g_ref, wgu_ref, wdn_ref, o_ref):
    _ffn_body(h_ref[...], g_ref, wgu_ref, wdn_ref, o_ref)


def _proj_ffn_kernel(h_ref, a_ref, wo_ref, g_ref, wgu_ref, wdn_ref, o_ref):
    h = h_ref[...] + _dot(a_ref[...], wo_ref[...])
    _ffn_body(h, g_ref, wgu_ref, wdn_ref, o_ref)


def _ffn(h2, g, w_gu, w_dn):
    t, d = h2.shape
    tm = TM_FFN
    tok = pl.BlockSpec((tm, d), lambda i: (i, 0))
    return pl.pallas_call(
        _ffn_kernel,
        out_shape=jax.ShapeDtypeStruct(h2.shape, h2.dtype),
        grid=(t // tm,),
        in_specs=[tok, _const_spec(g.shape), _const_spec(w_gu.shape), _const_spec(w_dn.shape)],
        out_specs=tok,
        compiler_params=pltpu.CompilerParams(
            dimension_semantics=("arbitrary",), vmem_limit_bytes=VMEM_LIMIT),
        name="ffn",
    )(h2, g, w_gu, w_dn)


def _proj_ffn(h_tm, attn, w_o, g, w_gu, w_dn):
    t, da = attn.shape
    d = w_o.shape[1]
    seq = h_tm.shape[0]
    tm = TM_FFN
    ns = seq // tm
    tok = pl.BlockSpec((tm, d), lambda i: (i, 0))
    return pl.pallas_call(
        _proj_ffn_kernel,
        out_shape=jax.ShapeDtypeStruct((t, d), h_tm.dtype),
        grid=(t // tm,),
        in_specs=[pl.BlockSpec((tm, d), lambda i: (i % ns, i // ns)),
                  pl.BlockSpec((tm, da), lambda i: (i, 0)),
                  _const_spec(w_o.shape), _const_spec(g.shape), _const_spec(w_gu.shape),
                  _const_spec(w_dn.shape)],
        out_specs=tok,
        compiler_params=pltpu.CompilerParams(
            dimension_semantics=("arbitrary",), vmem_limit_bytes=VMEM_LIMIT),
        name="proj_ffn",
    )(h_tm, attn, w_o, g, w_gu, w_dn)


def _row(v):
    return v.reshape(1, -1)


def _pad_lanes(v, n):
    return jnp.pad(v, [(0, 0)] * (v.ndim - 1) + [(0, n - v.shape[-1])])


def _dup(v):
    return jnp.concatenate([v, v], axis=-1)


def kernel(x, positions, mix_norm, conv_w_in, conv_w, conv_w_out, lru_w_in, lru_conv_w, lru_conv_b, lru_gate_a_w, lru_gate_a_b, lru_gate_x_w, lru_gate_x_b, lru_lambda, lru_w_out, mla_w_down, mla_q_norm, mla_kv_norm, mla_w_uq, mla_w_ukv, mla_qn_norm, mla_qr_norm, mla_kn_norm, mla_kr_norm, mla_w_o, ffn_norm, ffn_w_gu, ffn_w_down):
    bsz, seq, d = x.shape
    nh = MLA_HEADS
    half = QK_ROPE_DIM // 2

    inv_freq = ROPE_THETA ** (-jnp.arange(0, QK_ROPE_DIM, 2, dtype=F32) / QK_ROPE_DIM)
    freq_slab = _row(_pad_lanes(_dup(inv_freq), LANES))
    sign_slab = _row(_pad_lanes(jnp.concatenate([-jnp.ones(half, F32), jnp.ones(half, F32)]), LANES))
    live_slab = _row(_pad_lanes(jnp.ones(QK_ROPE_DIM, F32), LANES))
    posb = jnp.broadcast_to(positions.astype(F32)[..., None], (bsz, seq, LANES))

    h = x
    time_major = False
    for i in range(DEPTH):
        kind, j = i % N_MIXERS, i // N_MIXERS
        g = _row(mix_norm[i])
        gf, wgu, wdn = _row(ffn_norm[i]), ffn_w_gu[i].astype(BF16), ffn_w_down[i].astype(BF16)
        if kind == 0:
            assert not time_major
            time_major = i + 1 < DEPTH and (i + 1) % N_MIXERS == 1
            h = _conv_layer(h.reshape(bsz, seq, d), g, conv_w_in[j].astype(BF16), conv_w[j],
                            conv_w_out[j].astype(BF16), time_major)
            h = _ffn(h.reshape(bsz * seq, d), gf, wgu, wdn)
        elif kind == 1:
            assert time_major
            wax = jnp.concatenate([lru_gate_a_w[j], lru_gate_x_w[j]], axis=-1).astype(BF16)
            h = _lru_layer(h.reshape(seq, bsz, d), g, lru_w_in[j].astype(BF16), lru_conv_w[j],
                           _row(lru_conv_b[j]), wax, _row(lru_gate_a_b[j]), _row(lru_gate_x_b[j]),
                           _row(lru_lambda[j]), lru_w_out[j].astype(BF16))
            h = _ffn(h.reshape(bsz * seq, d), gf, wgu, wdn)
        else:
            assert time_major
            wdm = mla_w_down[j]
            wd = jnp.concatenate([wdm, wdm[:, Q_LORA_RANK + KV_LORA_RANK:]], axis=1).astype(BF16)
            wq = mla_w_uq[j].reshape(Q_LORA_RANK, nh, QK_NOPE_DIM + QK_ROPE_DIM)
            wuq = jnp.concatenate(
                [wq[:, :, :QK_NOPE_DIM].reshape(Q_LORA_RANK, nh * QK_NOPE_DIM),
                 _dup(wq[:, :, QK_NOPE_DIM:]).reshape(Q_LORA_RANK, nh * LANES)],
                axis=1).astype(BF16)
            wkv = mla_w_ukv[j].reshape(KV_LORA_RANK, nh, QK_NOPE_DIM + V_HEAD_DIM)
            wukv = jnp.concatenate(
                [wkv[:, :, :QK_NOPE_DIM].reshape(KV_LORA_RANK, nh * QK_NOPE_DIM),
                 wkv[:, :, QK_NOPE_DIM:].reshape(KV_LORA_RANK, nh * V_HEAD_DIM)], axis=1).astype(BF16)
            consts = (g, wd, _row(mla_q_norm[j]), _row(mla_kv_norm[j]), wuq, wukv,
                      _row(mla_qn_norm[j]), _row(_dup(mla_qr_norm[j])),
                      _row(mla_kn_norm[j]), _row(_dup(mla_kr_norm[j])),
                      freq_slab, sign_slab, live_slab)
            h_tm = h.reshape(seq, bsz * d)
            q, k, v1 = _mla_proj(h_tm, posb, consts)
            attn = _attention(q, k, v1).reshape(bsz * seq, nh * V_HEAD_DIM)
            h = _proj_ffn(h_tm, attn, mla_w_o[j].astype(BF16), gf, wgu, wdn)
            time_major = False
    assert not time_major
    return h.reshape(bsz, seq, d)
```

```python
import math

import jax
import jax.numpy as jnp
from jax import lax
from jax.experimental import pallas as pl
from jax.experimental.pallas import tpu as pltpu

F32 = jnp.float32
BF16 = jnp.bfloat16

D_MODEL = 1024
DEPTH = 4
N_MIXERS = 3
CONV_WIDTH = 3
LRU_WIDTH = 1280
LRU_BLOCKS = 10
LRU_BLOCK_W = LRU_WIDTH // LRU_BLOCKS
LRU_CONV_WIDTH = 4
LRU_C = 8.0
MLA_HEADS = 8
Q_LORA_RANK = 384
KV_LORA_RANK = 256
QK_NOPE_DIM = 128
QK_ROPE_DIM = 64
V_HEAD_DIM = 128
ROPE_THETA = 10000.0
D_FF = 2816
NORM_EPS = 1e-6

LANES = 128
SUBLANES = 8
QK_DIM_PADDED = 2 * LANES
VMEM_LIMIT = 56 * 1024 * 1024

TS_CONV = 512
TT_LRU = 128
LRU_SUBTILES = 4
TS_MLA = 512
TQ_ATTN = 512
ATTN_HEADS_PER_STEP = 4
TM_FFN = 512
MXU_DIM = 256
_FFN_SPLIT = (D_FF // MXU_DIM + 1) // 2 * MXU_DIM
FFN_CHUNK_BOUNDS = ((0, _FFN_SPLIT), (_FFN_SPLIT, D_FF))
LOG2E = 1.4426950408889634


def _rms(x, g):
    return x * lax.rsqrt(jnp.mean(x * x, axis=-1, keepdims=True) + NORM_EPS) * g


def _dot(a, b):
    return jnp.dot(a, b, preferred_element_type=F32)


def _gelu_tanh(x):
    k0 = -2.0 * LOG2E * math.sqrt(2.0 / math.pi)
    return x / (1.0 + jnp.exp2(x * (k0 + (k0 * 0.044715) * (x * x))))


def _const_spec(shape):
    nd = len(shape)
    return pl.BlockSpec(shape, lambda *_: (0,) * nd, pipeline_mode=pl.Buffered(1))


def _shift_rows_prev(x, prev8, d):
    r = pltpu.roll(x, d, 0)
    hd = pltpu.roll(prev8, d, 0)
    rows8 = lax.broadcasted_iota(jnp.int32, prev8.shape, 0)
    first = jnp.where(rows8 < d, hd, r[:SUBLANES])
    return jnp.concatenate([first, r[SUBLANES:]], axis=0)


def _conv_layer_kernel(h_ref, g_ref, win_ref, cw_ref, wout_ref, o_ref, carry_ref):
    @pl.when(pl.program_id(1) == 0)
    def _():
        carry_ref[...] = jnp.zeros_like(carry_ref)

    h = h_ref[0]
    xn = _rms(h, g_ref[...]).astype(BF16)
    bch = _dot(xn, win_ref[...])
    b_gate = bch[:, :D_MODEL]
    u = bch[:, D_MODEL:2 * D_MODEL] * bch[:, 2 * D_MODEL:]
    prev8 = carry_ref[...]
    cw = cw_ref[...]
    conv = (cw[0:1] * _shift_rows_prev(u, prev8, 2)
            + cw[1:2] * _shift_rows_prev(u, prev8, 1)
            + cw[2:3] * u)
    carry_ref[...] = u[u.shape[0] - SUBLANES:]
    y = _dot((b_gate * conv).astype(BF16), wout_ref[...])
    o_ref[0] = h + y


def _conv_layer(h, g, w_in, cw, w_out):
    bsz, seq, d = h.shape
    ts = TS_CONV
    tok = pl.BlockSpec((1, ts, d), lambda b, s: (b, s, 0))
    return pl.pallas_call(
        _conv_layer_kernel,
        out_shape=jax.ShapeDtypeStruct(h.shape, h.dtype),
        grid=(bsz, seq // ts),
        in_specs=[
            tok, _const_spec(g.shape), _const_spec(w_in.shape), _const_spec(cw.shape),
            _const_spec(w_out.shape),
        ],
        out_specs=tok,
        scratch_shapes=[pltpu.VMEM((SUBLANES, d), F32)],
        compiler_params=pltpu.CompilerParams(
            dimension_semantics=("arbitrary", "arbitrary"), vmem_limit_bytes=VMEM_LIMIT),
        name="conv_layer",
    )(h, g, w_in, cw, w_out)


def _conv_layer_tm_kernel(h_ref, g_ref, win_ref, cw_ref, wout_ref, o_ref, carry_ref):
    @pl.when(pl.program_id(0) == 0)
    def _():
        carry_ref[...] = jnp.zeros_like(carry_ref)

    nb, tt, d = h_ref.shape
    rows = tt * nb
    h = jnp.swapaxes(h_ref[...], 0, 1).reshape(rows, d)
    xn = _rms(h, g_ref[...]).astype(BF16)
    bch = _dot(xn, win_ref[...])
    b_gate = bch[:, :D_MODEL]
    u = bch[:, D_MODEL:2 * D_MODEL] * bch[:, 2 * D_MODEL:]
    halo = (CONV_WIDTH - 1) * nb
    ext = jnp.concatenate([carry_ref[...], u], axis=0)
    carry_ref[...] = u[rows - halo:]
    cw = cw_ref[...]
    conv = cw[CONV_WIDTH - 1:CONV_WIDTH] * u
    for k in range(CONV_WIDTH - 1):
        conv = conv + cw[k:k + 1] * ext[k * nb:k * nb + rows]
    y = _dot((b_gate * conv).astype(BF16), wout_ref[...])
    o_ref[...] = (h + y).reshape(tt, nb, d)


def _conv_layer_to_time_major(h, g, w_in, cw, w_out):
    bsz, seq, d = h.shape
    assert bsz == SUBLANES
    tt = TS_CONV // bsz
    return pl.pallas_call(
        _conv_layer_tm_kernel,
        out_shape=jax.ShapeDtypeStruct((seq, bsz, d), h.dtype),
        grid=(seq // tt,),
        in_specs=[
            pl.BlockSpec((bsz, tt, d), lambda s: (0, s, 0)),
            _const_spec(g.shape), _const_spec(w_in.shape), _const_spec(cw.shape),
            _const_spec(w_out.shape),
        ],
        out_specs=pl.BlockSpec((tt, bsz, d), lambda s: (s, 0, 0)),
        scratch_shapes=[pltpu.VMEM(((CONV_WIDTH - 1) * bsz, d), F32)],
        compiler_params=pltpu.CompilerParams(
            dimension_semantics=("arbitrary",), vmem_limit_bytes=VMEM_LIMIT),
        name="conv_layer_tm",
    )(h, g, w_in, cw, w_out)


def _lru_layer_kernel(h_ref, g_ref, win_ref, cw_ref, cb_ref, wax_ref, ba_ref, bx_ref, lam_ref,
                      wout_ref, o_ref, xcarry_ref, hcarry_ref):
    @pl.when(pl.program_id(0) == 0)
    def _():
        xcarry_ref[...] = jnp.zeros_like(xcarry_ref)
        hcarry_ref[...] = jnp.zeros_like(hcarry_ref)

    tt, nb, d = h_ref.shape
    ts = tt // LRU_SUBTILES
    rows = ts * nb
    halo = (LRU_CONV_WIDTH - 1) * nb
    cw = cw_ref[...]
    lam = lam_ref[...]
    log_sig_lam = jnp.minimum(lam, 0.0) - jnp.log1p(jnp.exp(-jnp.abs(lam)))
    neg_half_c_lsl = (-0.5 * LRU_C) * log_sig_lam
    ba = ba_ref[...]
    bx = bx_ref[...]
    x_prev = xcarry_ref[...]
    h_prev = hcarry_ref[...]

    def project_in(sub):
        h = h_ref[sub * ts:(sub + 1) * ts].reshape(rows, d)
        xn = _rms(h, g_ref[...]).astype(BF16)
        return h, _dot(xn, win_ref[...])

    projected = project_in(0)
    for sub in range(LRU_SUBTILES):
        t0 = sub * ts
        h, gr = projected
        if sub + 1 < LRU_SUBTILES:
            projected = project_in(sub + 1)
        gate = _gelu_tanh(gr[:, :LRU_WIDTH])
        xr = gr[:, LRU_WIDTH:]
        ext = jnp.concatenate([x_prev, xr], axis=0)
        x_prev = xr[rows - halo:]
        rec = cb_ref[...] + cw[LRU_CONV_WIDTH - 1:LRU_CONV_WIDTH] * xr
        for k in range(LRU_CONV_WIDTH - 1):
            rec = rec + cw[k:k + 1] * ext[k * nb:k * nb + rows]

        outs = []
        for n in range(LRU_BLOCKS):
            sl = slice(n * LRU_BLOCK_W, (n + 1) * LRU_BLOCK_W)
            xb = rec[:, sl]
            ri = _dot(xb.astype(BF16), wax_ref[n])
            neg_log_a = neg_half_c_lsl[:, sl] * (jnp.tanh(ri[:, :LRU_BLOCK_W] + ba[:, sl]) + 1.0)
            i = 0.5 * jnp.tanh(ri[:, LRU_BLOCK_W:] + bx[:, sl]) + 0.5
            a = jnp.exp2(neg_log_a * (-LOG2E))
            z = jnp.tanh(neg_log_a) * (a * a + 1.0)
            mult = jnp.where(z > 0.0, z * lax.rsqrt(z), 0.0)
            b = mult * (i * xb)
            hp = h_prev[:, sl]
            steps = []
            for t in range(ts):
                hp = a[t * nb:(t + 1) * nb] * hp + b[t * nb:(t + 1) * nb]
                steps.append(hp)
            outs.append(jnp.concatenate(steps, axis=0))
        hs = jnp.concatenate(outs, axis=1)
        h_prev = hs[rows - nb:]
        y = _dot((gate * hs).astype(BF16), wout_ref[...])
        o_ref[t0:t0 + ts] = (h + y).reshape(ts, nb, d)

    xcarry_ref[...] = x_prev
    hcarry_ref[...] = h_prev


def _lru_layer(h, g, w_in, cw, cb, wax, ba, bx, lam, w_out):
    seq, bsz, d = h.shape
    assert bsz == SUBLANES
    tt = TT_LRU
    consts = (g, w_in, cw, cb, wax, ba, bx, lam, w_out)
    tok = pl.BlockSpec((tt, bsz, d), lambda s: (s, 0, 0))
    return pl.pallas_call(
        _lru_layer_kernel,
        out_shape=jax.ShapeDtypeStruct(h.shape, h.dtype),
        grid=(seq // tt,),
        in_specs=[tok] + [_const_spec(c.shape) for c in consts],
        out_specs=tok,
        scratch_shapes=[pltpu.VMEM(((LRU_CONV_WIDTH - 1) * bsz, LRU_WIDTH), F32),
                        pltpu.VMEM((bsz, LRU_WIDTH), F32)],
        compiler_params=pltpu.CompilerParams(
            dimension_semantics=("arbitrary",), vmem_limit_bytes=VMEM_LIMIT),
        name="lru_layer",
    )(h, *consts)


def _mla_proj_kernel(h_ref, pos_ref, g_ref, wd_ref, qg_ref, kvg_ref, wuq_ref, wukv_ref,
                     qng_ref, qrg_ref, kng_ref, krg_ref, freq_ref, sign_ref, live_ref,
                     q_ref, k_ref, v_ref):
    h = h_ref[0]
    xn = _rms(h, g_ref[...]).astype(BF16)
    c = _dot(xn, wd_ref[...])
    c_q = c[:, :Q_LORA_RANK]
    c_kv = c[:, Q_LORA_RANK:Q_LORA_RANK + KV_LORA_RANK]
    k_rope = c[:, Q_LORA_RANK + KV_LORA_RANK:]
    q = _dot(_rms(c_q, qg_ref[...]).astype(BF16), wuq_ref[...])
    kv = _dot(_rms(c_kv, kvg_ref[...]).astype(BF16), wukv_ref[...])

    ang = pos_ref[0] * freq_ref[...]
    cos = jnp.cos(ang) * live_ref[...]
    sin = jnp.sin(ang) * sign_ref[...]
    half = QK_ROPE_DIM // 2

    def rope_slab(x, gain):
        ms = jnp.sum(x * x, axis=-1, keepdims=True) * (1.0 / (2 * QK_ROPE_DIM))
        y = x * lax.rsqrt(ms + NORM_EPS) * gain
        return y * cos + pltpu.roll(y, half, 1) * sin

    q_scale = LOG2E / math.sqrt(QK_NOPE_DIM + QK_ROPE_DIM)
    k_rope = rope_slab(k_rope, krg_ref[...]).astype(BF16)
    ones = jnp.ones((h.shape[0], LANES), BF16)
    nope_all = MLA_HEADS * QK_NOPE_DIM
    for hh in range(MLA_HEADS):
        sl = slice(hh * LANES, (hh + 1) * LANES)
        sl2 = slice(nope_all + hh * LANES, nope_all + (hh + 1) * LANES)
        q_nope = _rms(q[:, sl], qng_ref[...])
        q_rope = rope_slab(q[:, sl2], qrg_ref[...])
        q_ref[0, hh, :, :LANES] = (q_nope * q_scale).astype(BF16)
        q_ref[0, hh, :, LANES:] = (q_rope * q_scale).astype(BF16)
        k_ref[0, hh, :, :LANES] = _rms(kv[:, sl], kng_ref[...]).astype(BF16)
        k_ref[0, hh, :, LANES:] = k_rope
        v_ref[0, hh, :, :LANES] = kv[:, sl2].astype(BF16)
        v_ref[0, hh, :, LANES:] = ones


def _mla_proj(h, posb, consts):
    bsz, seq, d = h.shape
    ts = TS_MLA
    out_sds = jax.ShapeDtypeStruct((bsz, MLA_HEADS, seq, QK_DIM_PADDED), BF16)
    out_spec = pl.BlockSpec((1, MLA_HEADS, ts, QK_DIM_PADDED), lambda b, s: (b, 0, s, 0))
    return pl.pallas_call(
        _mla_proj_kernel,
        out_shape=(out_sds, out_sds, out_sds),
        grid=(bsz, seq // ts),
        in_specs=[pl.BlockSpec((1, ts, d), lambda b, s: (b, s, 0)),
                  pl.BlockSpec((1, ts, LANES), lambda b, s: (b, s, 0))]
        + [_const_spec(c.shape) for c in consts],
        out_specs=(out_spec, out_spec, out_spec),
        compiler_params=pltpu.CompilerParams(
            dimension_semantics=("arbitrary", "arbitrary"), vmem_limit_bytes=VMEM_LIMIT),
        name="mla_proj",
    )(h, posb, *consts)


def _attn_kernel(q_ref, k_ref, v_ref, o_ref, m_ref, acc_ref, s0_ref, s1_ref):
    qi = pl.program_id(2)
    nhp, tq = q_ref.shape[1], q_ref.shape[2]
    tk = tq
    ng = nhp // 2
    m_ref[...] = jnp.full_like(m_ref, -jnp.inf)
    acc_ref[...] = jnp.zeros_like(acc_ref)

    def logits(kb, group, s_ref):
        start = pl.multiple_of(kb * tk, tk)
        for i in range(ng):
            hp = group * ng + i
            k = k_ref[0, hp, pl.ds(start, tk), :]
            s_ref[i] = lax.dot_general(q_ref[0, hp], k, (((1,), (1,)), ((), ())),
                                       preferred_element_type=F32)

    def softmax_pv(kb, group, s_ref, masked):
        start = pl.multiple_of(kb * tk, tk)
        for i in range(ng):
            hp = group * ng + i
            s = s_ref[i]
            if masked:
                row = lax.broadcasted_iota(jnp.int32, s.shape, 0)
                col = lax.broadcasted_iota(jnp.int32, s.shape, 1)
                s = jnp.where(col <= row, s, jnp.finfo(F32).min)
            m_prev = m_ref[hp]
            m_next = jnp.maximum(m_prev, jnp.max(s, axis=1, keepdims=True))
            alpha = jnp.exp2(m_prev - m_next)
            p = jnp.exp2(s - jnp.concatenate([m_next] * (tk // LANES), axis=1))
            v = v_ref[0, hp, pl.ds(start, tk), :]
            acc_ref[hp] = (jnp.concatenate([alpha, alpha], axis=1) * acc_ref[hp]
                           + _dot(p.astype(BF16), v))
            m_ref[hp] = m_next

    logits(0, 0, s0_ref)

    def body(kb, carry):
        logits(kb, 1, s1_ref)
        softmax_pv(kb, 0, s0_ref, False)
        logits(kb + 1, 0, s0_ref)
        softmax_pv(kb, 1, s1_ref, False)
        return carry

    lax.fori_loop(0, qi, body, 0)
    logits(qi, 1, s1_ref)
    softmax_pv(qi, 0, s0_ref, True)
    softmax_pv(qi, 1, s1_ref, True)

    for hp in range(nhp):
        acc = acc_ref[hp]
        o_ref[0, :, hp * LANES:(hp + 1) * LANES] = (acc[:, :LANES] / acc[:, LANES:]).astype(o_ref.dtype)


def _attention(q, k, v1):
    bsz, nh, seq, dq = q.shape
    tq = TQ_ATTN
    hp = ATTN_HEADS_PER_STEP
    return pl.pallas_call(
        _attn_kernel,
        out_shape=jax.ShapeDtypeStruct((bsz, seq, nh * V_HEAD_DIM), BF16),
        grid=(bsz, nh // hp, seq // tq),
        in_specs=[pl.BlockSpec((1, hp, tq, dq), lambda b, h, i: (b, h, i, 0)),
                  pl.BlockSpec((1, hp, seq, dq), lambda b, h, i: (b, h, 0, 0)),
                  pl.BlockSpec((1, hp, seq, dq), lambda b, h, i: (b, h, 0, 0))],
        out_specs=pl.BlockSpec((1, tq, hp * V_HEAD_DIM), lambda b, h, i: (b, i, h)),
        scratch_shapes=[pltpu.VMEM((hp, tq, LANES), F32), pltpu.VMEM((hp, tq, dq), F32),
                        pltpu.VMEM((hp // 2, tq, tq), F32), pltpu.VMEM((hp // 2, tq, tq), F32)],
        compiler_params=pltpu.CompilerParams(
            dimension_semantics=("arbitrary", "arbitrary", "arbitrary"),
            vmem_limit_bytes=VMEM_LIMIT),
        name="mla_attention",
    )(q, k, v1)


def _ffn_body(h, g_ref, wgu_ref, wdn_ref):
    xn = _rms(h, g_ref[...]).astype(BF16)
    out = h
    for lo, hi in FFN_CHUNK_BOUNDS:
        gate = _dot(xn, wgu_ref[:, lo:hi])
        up = _dot(xn, wgu_ref[:, D_FF + lo:D_FF + hi])
        act = (jax.nn.silu(gate) * up).astype(BF16)
        out = out + _dot(act, wdn_ref[lo:hi, :])
    return out


def _ffn_kernel(h_ref, g_ref, wgu_ref, wdn_ref, o_ref):
    o_ref[...] = _ffn_body(h_ref[...], g_ref, wgu_ref, wdn_ref)


def _ffn_from_time_major_kernel(h_ref, g_ref, wgu_ref, wdn_ref, o_ref):
    tt, nb, d = h_ref.shape
    out = _ffn_body(h_ref[...].reshape(tt * nb, d), g_ref, wgu_ref, wdn_ref)
    o_ref[...] = jnp.swapaxes(out.reshape(tt, nb, d), 0, 1)


def _proj_ffn_kernel(h_ref, a_ref, wo_ref, g_ref, wgu_ref, wdn_ref, o_ref):
    h = h_ref[...] + _dot(a_ref[...], wo_ref[...])
    o_ref[...] = _ffn_body(h, g_ref, wgu_ref, wdn_ref)


def _ffn_from_time_major(h, g, w_gu, w_dn):
    seq, bsz, d = h.shape
    tt = TM_FFN // bsz
    return pl.pallas_call(
        _ffn_from_time_major_kernel,
        out_shape=jax.ShapeDtypeStruct((bsz, seq, d), h.dtype),
        grid=(seq // tt,),
        in_specs=[pl.BlockSpec((tt, bsz, d), lambda s: (s, 0, 0)),
                  _const_spec(g.shape), _const_spec(w_gu.shape), _const_spec(w_dn.shape)],
        out_specs=pl.BlockSpec((bsz, tt, d), lambda s: (0, s, 0)),
        compiler_params=pltpu.CompilerParams(
            dimension_semantics=("arbitrary",), vmem_limit_bytes=VMEM_LIMIT),
        name="ffn_tm",
    )(h, g, w_gu, w_dn)


def _ffn(h2, g, w_gu, w_dn):
    t, d = h2.shape
    tm = TM_FFN
    tok = pl.BlockSpec((tm, d), lambda i: (i, 0))
    return pl.pallas_call(
        _ffn_kernel,
        out_shape=jax.ShapeDtypeStruct(h2.shape, h2.dtype),
        grid=(t // tm,),
        in_specs=[tok, _const_spec(g.shape), _const_spec(w_gu.shape), _const_spec(w_dn.shape)],
        out_specs=tok,
        compiler_params=pltpu.CompilerParams(
            dimension_semantics=("arbitrary",), vmem_limit_bytes=VMEM_LIMIT),
        name="ffn",
    )(h2, g, w_gu, w_dn)


def _proj_ffn(h2, attn, w_o, g, w_gu, w_dn):
    t, d = h2.shape
    da = attn.shape[1]
    tm = TM_FFN
    tok = pl.BlockSpec((tm, d), lambda i: (i, 0))
    return pl.pallas_call(
        _proj_ffn_kernel,
        out_shape=jax.ShapeDtypeStruct((t, d), h2.dtype),
        grid=(t // tm,),
        in_specs=[tok,
                  pl.BlockSpec((tm, da), lambda i: (i, 0)),
                  _const_spec(w_o.shape), _const_spec(g.shape), _const_spec(w_gu.shape),
                  _const_spec(w_dn.shape)],
        out_specs=tok,
        compiler_params=pltpu.CompilerParams(
            dimension_semantics=("arbitrary",), vmem_limit_bytes=VMEM_LIMIT),
        name="proj_ffn",
    )(h2, attn, w_o, g, w_gu, w_dn)


def _row(v):
    return v.reshape(1, -1)


def _pad_lanes(v, n):
    return jnp.pad(v, [(0, 0)] * (v.ndim - 1) + [(0, n - v.shape[-1])])


def _dup(v):
    return jnp.concatenate([v, v], axis=-1)


def kernel(x, positions, mix_norm, conv_w_in, conv_w, conv_w_out, lru_w_in, lru_conv_w, lru_conv_b, lru_gate_a_w, lru_gate_a_b, lru_gate_x_w, lru_gate_x_b, lru_lambda, lru_w_out, mla_w_down, mla_q_norm, mla_kv_norm, mla_w_uq, mla_w_ukv, mla_qn_norm, mla_qr_norm, mla_kn_norm, mla_kr_norm, mla_w_o, ffn_norm, ffn_w_gu, ffn_w_down):
    bsz, seq, d = x.shape
    nh = MLA_HEADS
    half = QK_ROPE_DIM // 2

    inv_freq = ROPE_THETA ** (-jnp.arange(0, QK_ROPE_DIM, 2, dtype=F32) / QK_ROPE_DIM)
    freq_slab = _row(_pad_lanes(_dup(inv_freq), LANES))
    sign_slab = _row(_pad_lanes(jnp.concatenate([-jnp.ones(half, F32), jnp.ones(half, F32)]), LANES))
    live_slab = _row(_pad_lanes(jnp.ones(QK_ROPE_DIM, F32), LANES))
    posb = jnp.broadcast_to(positions.astype(F32)[..., None], (bsz, seq, LANES))

    h = x
    for i in range(DEPTH):
        kind, j = i % N_MIXERS, i // N_MIXERS
        g = _row(mix_norm[i])
        gf, wgu, wdn = _row(ffn_norm[i]), ffn_w_gu[i].astype(BF16), ffn_w_down[i].astype(BF16)
        if kind == 0:
            conv_args = (h, g, conv_w_in[j].astype(BF16), conv_w[j], conv_w_out[j].astype(BF16))
            if i + 1 < DEPTH and (i + 1) % N_MIXERS == 1:
                h = _conv_layer_to_time_major(*conv_args)
                h = _ffn(h.reshape(seq * bsz, d), gf, wgu, wdn).reshape(seq, bsz, d)
            else:
                h = _conv_layer(*conv_args)
                h = _ffn(h.reshape(bsz * seq, d), gf, wgu, wdn).reshape(bsz, seq, d)
        elif kind == 1:
            wax = (0.5 * jnp.concatenate([lru_gate_a_w[j], lru_gate_x_w[j]], axis=-1)).astype(BF16)
            h = _lru_layer(h, g, lru_w_in[j].astype(BF16), lru_conv_w[j], _row(lru_conv_b[j]),
                           wax, _row(0.5 * lru_gate_a_b[j]), _row(0.5 * lru_gate_x_b[j]),
                           _row(lru_lambda[j]), lru_w_out[j].astype(BF16))
            h = _ffn_from_time_major(h, gf, wgu, wdn)
        else:
            wdm = mla_w_down[j]
            wd = jnp.concatenate([wdm, wdm[:, Q_LORA_RANK + KV_LORA_RANK:]], axis=1).astype(BF16)
            wq = mla_w_uq[j].reshape(Q_LORA_RANK, nh, QK_NOPE_DIM + QK_ROPE_DIM)
            wuq = jnp.concatenate(
                [wq[:, :, :QK_NOPE_DIM].reshape(Q_LORA_RANK, nh * QK_NOPE_DIM),
                 _dup(wq[:, :, QK_NOPE_DIM:]).reshape(Q_LORA_RANK, nh * LANES)],
                axis=1).astype(BF16)
            wkv = mla_w_ukv[j].reshape(KV_LORA_RANK, nh, QK_NOPE_DIM + V_HEAD_DIM)
            wukv = jnp.concatenate(
                [wkv[:, :, :QK_NOPE_DIM].reshape(KV_LORA_RANK, nh * QK_NOPE_DIM),
                 wkv[:, :, QK_NOPE_DIM:].reshape(KV_LORA_RANK, nh * V_HEAD_DIM)], axis=1).astype(BF16)
            consts = (g, wd, _row(mla_q_norm[j]), _row(mla_kv_norm[j]), wuq, wukv,
                      _row(mla_qn_norm[j]), _row(_dup(mla_qr_norm[j])),
                      _row(mla_kn_norm[j]), _row(_dup(mla_kr_norm[j])),
                      freq_slab, sign_slab, live_slab)
            q, k, v1 = _mla_proj(h, posb, consts)
            attn = _attention(q, k, v1).reshape(bsz * seq, nh * V_HEAD_DIM)
            h = _proj_ffn(h.reshape(bsz * seq, d), attn, mla_w_o[j].astype(BF16), gf, wgu,
                          wdn).reshape(bsz, seq, d)
    return h
```

```python
import math

import jax
import jax.numpy as jnp
from jax import lax
from jax.experimental import pallas as pl
from jax.experimental.pallas import tpu as pltpu

F32 = jnp.float32
BF16 = jnp.bfloat16

D_MODEL = 1024
DEPTH = 4
N_MIXERS = 3
CONV_WIDTH = 3
LRU_WIDTH = 1280
LRU_BLOCKS = 10
LRU_BLOCK_W = LRU_WIDTH // LRU_BLOCKS
LRU_CONV_WIDTH = 4
LRU_C = 8.0
MLA_HEADS = 8
Q_LORA_RANK = 384
KV_LORA_RANK = 256
QK_NOPE_DIM = 128
QK_ROPE_DIM = 64
V_HEAD_DIM = 128
ROPE_THETA = 10000.0
D_FF = 2816
NORM_EPS = 1e-6

LANES = 128
SUBLANES = 8
QK_DIM_PADDED = 2 * LANES
VMEM_LIMIT = 56 * 1024 * 1024

TS_CONV = 1024
CONV_SUBTILES = 2
TT_LRU = 128
LRU_SUBTILES = 4
TS_MLA = 1024
MLA_SUBTILES = 4
TQ_ATTN = 512
ATTN_HEADS_PER_STEP = 4
TM_FFN = 1024
FFN_SUBTILES = 2
MXU_DIM = 256
_FFN_SPLIT = (D_FF // MXU_DIM + 1) // 2 * MXU_DIM
FFN_CHUNK_BOUNDS = ((0, _FFN_SPLIT), (_FFN_SPLIT, D_FF))
LOG2E = 1.4426950408889634


def _rms(x, g):
    return x * lax.rsqrt(jnp.mean(x * x, axis=-1, keepdims=True) + NORM_EPS) * g


def _dot(a, b):
    return jnp.dot(a, b, preferred_element_type=F32)


def _gelu_tanh(x):
    k0 = -2.0 * LOG2E * math.sqrt(2.0 / math.pi)
    return x / (1.0 + jnp.exp2(x * (k0 + (k0 * 0.044715) * (x * x))))


def _const_spec(shape):
    nd = len(shape)
    return pl.BlockSpec(shape, lambda *_: (0,) * nd, pipeline_mode=pl.Buffered(1))


def _shift_rows_prev(x, prev8, d):
    r = pltpu.roll(x, d, 0)
    hd = pltpu.roll(prev8, d, 0)
    rows8 = lax.broadcasted_iota(jnp.int32, prev8.shape, 0)
    first = jnp.where(rows8 < d, hd, r[:SUBLANES])
    return jnp.concatenate([first, r[SUBLANES:]], axis=0)


def _conv_layer_kernel(h_ref, g_ref, win_ref, cw_ref, wout_ref, o_ref, carry_ref):
    @pl.when(pl.program_id(1) == 0)
    def _():
        carry_ref[...] = jnp.zeros_like(carry_ref)

    rows = h_ref.shape[1] // CONV_SUBTILES
    cw = cw_ref[...]

    def project_in(sub):
        h = h_ref[0, sub * rows:(sub + 1) * rows, :]
        return h, _dot(_rms(h, g_ref[...]).astype(BF16), win_ref[...])

    prev8 = carry_ref[...]
    projected = project_in(0)
    for sub in range(CONV_SUBTILES):
        h, bch = projected
        if sub + 1 < CONV_SUBTILES:
            projected = project_in(sub + 1)
        b_gate = bch[:, :D_MODEL]
        u = bch[:, D_MODEL:2 * D_MODEL] * bch[:, 2 * D_MODEL:]
        conv = (cw[0:1] * _shift_rows_prev(u, prev8, 2)
                + cw[1:2] * _shift_rows_prev(u, prev8, 1)
                + cw[2:3] * u)
        prev8 = u[rows - SUBLANES:]
        y = _dot((b_gate * conv).astype(BF16), wout_ref[...])
        o_ref[0, sub * rows:(sub + 1) * rows, :] = h + y
    carry_ref[...] = prev8


def _conv_layer(h, g, w_in, cw, w_out):
    bsz, seq, d = h.shape
    ts = TS_CONV
    tok = pl.BlockSpec((1, ts, d), lambda b, s: (b, s, 0))
    return pl.pallas_call(
        _conv_layer_kernel,
        out_shape=jax.ShapeDtypeStruct(h.shape, h.dtype),
        grid=(bsz, seq // ts),
        in_specs=[
            tok, _const_spec(g.shape), _const_spec(w_in.shape), _const_spec(cw.shape),
            _const_spec(w_out.shape),
        ],
        out_specs=tok,
        scratch_shapes=[pltpu.VMEM((SUBLANES, d), F32)],
        compiler_params=pltpu.CompilerParams(
            dimension_semantics=("arbitrary", "arbitrary"), vmem_limit_bytes=VMEM_LIMIT),
        name="conv_layer",
    )(h, g, w_in, cw, w_out)


def _conv_layer_tm_kernel(h_ref, g_ref, win_ref, cw_ref, wout_ref, o_ref, carry_ref):
    @pl.when(pl.program_id(0) == 0)
    def _():
        carry_ref[...] = jnp.zeros_like(carry_ref)

    nb, tt, d = h_ref.shape
    ts = tt // CONV_SUBTILES
    rows = ts * nb
    halo = (CONV_WIDTH - 1) * nb
    cw = cw_ref[...]

    def project_in(sub):
        h = jnp.swapaxes(h_ref[:, sub * ts:(sub + 1) * ts, :], 0, 1).reshape(rows, d)
        return h, _dot(_rms(h, g_ref[...]).astype(BF16), win_ref[...])

    u_prev = carry_ref[...]
    projected = project_in(0)
    for sub in range(CONV_SUBTILES):
        h, bch = projected
        if sub + 1 < CONV_SUBTILES:
            projected = project_in(sub + 1)
        b_gate = bch[:, :D_MODEL]
        u = bch[:, D_MODEL:2 * D_MODEL] * bch[:, 2 * D_MODEL:]
        ext = jnp.concatenate([u_prev, u], axis=0)
        u_prev = u[rows - halo:]
        conv = cw[CONV_WIDTH - 1:CONV_WIDTH] * u
        for k in range(CONV_WIDTH - 1):
            conv = conv + cw[k:k + 1] * ext[k * nb:k * nb + rows]
        y = _dot((b_gate * conv).astype(BF16), wout_ref[...])
        o_ref[sub * ts:(sub + 1) * ts] = (h + y).reshape(ts, nb, d)
    carry_ref[...] = u_prev


def _conv_layer_to_time_major(h, g, w_in, cw, w_out):
    bsz, seq, d = h.shape
    assert bsz == SUBLANES
    tt = TS_CONV // bsz
    return pl.pallas_call(
        _conv_layer_tm_kernel,
        out_shape=jax.ShapeDtypeStruct((seq, bsz, d), h.dtype),
        grid=(seq // tt,),
        in_specs=[
            pl.BlockSpec((bsz, tt, d), lambda s: (0, s, 0)),
            _const_spec(g.shape), _const_spec(w_in.shape), _const_spec(cw.shape),
            _const_spec(w_out.shape),
        ],
        out_specs=pl.BlockSpec((tt, bsz, d), lambda s: (s, 0, 0)),
        scratch_shapes=[pltpu.VMEM(((CONV_WIDTH - 1) * bsz, d), F32)],
        compiler_params=pltpu.CompilerParams(
            dimension_semantics=("arbitrary",), vmem_limit_bytes=VMEM_LIMIT),
        name="conv_layer_tm",
    )(h, g, w_in, cw, w_out)


def _lru_layer_kernel(h_ref, g_ref, win_ref, cw_ref, cb_ref, wax_ref, ba_ref, bx_ref, lam_ref,
                      wout_ref, o_ref, xcarry_ref, hcarry_ref):
    @pl.when(pl.program_id(0) == 0)
    def _():
        xcarry_ref[...] = jnp.zeros_like(xcarry_ref)
        hcarry_ref[...] = jnp.zeros_like(hcarry_ref)

    tt, nb, d = h_ref.shape
    ts = tt // LRU_SUBTILES
    rows = ts * nb
    halo = (LRU_CONV_WIDTH - 1) * nb
    cw = cw_ref[...]
    lam = lam_ref[...]
    log_sig_lam = jnp.minimum(lam, 0.0) - jnp.log1p(jnp.exp(-jnp.abs(lam)))
    neg_half_c_lsl = (-0.5 * LRU_C) * log_sig_lam
    ba = ba_ref[...]
    bx = bx_ref[...]
    x_prev = xcarry_ref[...]
    h_prev = hcarry_ref[...]

    def project_in(sub):
        h = h_ref[sub * ts:(sub + 1) * ts].reshape(rows, d)
        xn = _rms(h, g_ref[...]).astype(BF16)
        return h, _dot(xn, win_ref[...])

    projected = project_in(0)
    for sub in range(LRU_SUBTILES):
        t0 = sub * ts
        h, gr = projected
        if sub + 1 < LRU_SUBTILES:
            projected = project_in(sub + 1)
        gate = _gelu_tanh(gr[:, :LRU_WIDTH])
        xr = gr[:, LRU_WIDTH:]
        ext = jnp.concatenate([x_prev, xr], axis=0)
        x_prev = xr[rows - halo:]
        rec = cb_ref[...] + cw[LRU_CONV_WIDTH - 1:LRU_CONV_WIDTH] * xr
        for k in range(LRU_CONV_WIDTH - 1):
            rec = rec + cw[k:k + 1] * ext[k * nb:k * nb + rows]

        outs = []
        for n in range(LRU_BLOCKS):
            sl = slice(n * LRU_BLOCK_W, (n + 1) * LRU_BLOCK_W)
            xb = rec[:, sl]
            ri = _dot(xb.astype(BF16), wax_ref[n])
            neg_log_a = neg_half_c_lsl[:, sl] * (jnp.tanh(ri[:, :LRU_BLOCK_W] + ba[:, sl]) + 1.0)
            i = 0.5 * jnp.tanh(ri[:, LRU_BLOCK_W:] + bx[:, sl]) + 0.5
            a = jnp.exp2(neg_log_a * (-LOG2E))
            z = jnp.tanh(neg_log_a) * (a * a + 1.0)
            mult = jnp.where(z > 0.0, z * lax.rsqrt(z), 0.0)
            b = mult * (i * xb)
            hp = h_prev[:, sl]
            steps = []
            for t in range(ts):
                hp = a[t * nb:(t + 1) * nb] * hp + b[t * nb:(t + 1) * nb]
                steps.append(hp)
            outs.append(jnp.concatenate(steps, axis=0))
        hs = jnp.concatenate(outs, axis=1)
        h_prev = hs[rows - nb:]
        y = _dot((gate * hs).astype(BF16), wout_ref[...])
        o_ref[t0:t0 + ts] = (h + y).reshape(ts, nb, d)

    xcarry_ref[...] = x_prev
    hcarry_ref[...] = h_prev


def _lru_layer(h, g, w_in, cw, cb, wax, ba, bx, lam, w_out):
    seq, bsz, d = h.shape
    assert bsz == SUBLANES
    tt = TT_LRU
    consts = (g, w_in, cw, cb, wax, ba, bx, lam, w_out)
    tok = pl.BlockSpec((tt, bsz, d), lambda s: (s, 0, 0))
    return pl.pallas_call(
        _lru_layer_kernel,
        out_shape=jax.ShapeDtypeStruct(h.shape, h.dtype),
        grid=(seq // tt,),
        in_specs=[tok] + [_const_spec(c.shape) for c in consts],
        out_specs=tok,
        scratch_shapes=[pltpu.VMEM(((LRU_CONV_WIDTH - 1) * bsz, LRU_WIDTH), F32),
                        pltpu.VMEM((bsz, LRU_WIDTH), F32)],
        compiler_params=pltpu.CompilerParams(
            dimension_semantics=("arbitrary",), vmem_limit_bytes=VMEM_LIMIT),
        name="lru_layer",
    )(h, *consts)


def _mla_proj_kernel(h_ref, pos_ref, g_ref, wd_ref, qg_ref, kvg_ref, wuq_ref, wukv_ref,
                     qng_ref, qrg_ref, kng_ref, krg_ref, freq_ref, sign_ref, live_ref, pair_ref,
                     q_ref, k_ref, v_ref):
    ts = h_ref.shape[1] // MLA_SUBTILES
    nope_all = MLA_HEADS * QK_NOPE_DIM
    half = QK_ROPE_DIM // 2
    q_scale = LOG2E / math.sqrt(QK_NOPE_DIM + QK_ROPE_DIM)
    qn_gain = qng_ref[...] * q_scale
    qr_gain = qrg_ref[...] * q_scale
    ones = jnp.ones((ts, LANES), BF16)

    def slab_norm_scales(x):
        scales = []
        for p in range(x.shape[1] // (2 * LANES)):
            xs = x[:, p * 2 * LANES:(p + 1) * 2 * LANES]
            r = lax.rsqrt(_dot((xs * xs).astype(BF16), pair_ref[...]) + NORM_EPS)
            scales += [r[:, :LANES], r[:, LANES:]]
        return scales

    def project(sub):
        rows = pl.ds(sub * ts, ts)
        xn = _rms(h_ref[0, rows, :], g_ref[...]).astype(BF16)
        c = _dot(xn, wd_ref[...])
        c_q = c[:, :Q_LORA_RANK]
        c_kv = c[:, Q_LORA_RANK:Q_LORA_RANK + KV_LORA_RANK]
        k_rope = c[:, Q_LORA_RANK + KV_LORA_RANK:]
        q = _dot(_rms(c_q, qg_ref[...]).astype(BF16), wuq_ref[...])
        kv = _dot(_rms(c_kv, kvg_ref[...]).astype(BF16), wukv_ref[...])
        rq_nope = slab_norm_scales(q[:, :nope_all])
        rq_rope = slab_norm_scales(q[:, nope_all:])
        rk_nope = slab_norm_scales(kv[:, :nope_all])
        rk_rope = lax.rsqrt(_dot((k_rope * k_rope).astype(BF16), pair_ref[:LANES, :LANES]) + NORM_EPS)
        return q, kv, k_rope, rq_nope, rq_rope, rk_nope, rk_rope

    def finish(sub, projected):
        q, kv, k_rope, rq_nope, rq_rope, rk_nope, rk_rope = projected
        rows = pl.ds(sub * ts, ts)
        ang = pos_ref[0, rows, :] * freq_ref[...]
        cos = jnp.cos(ang) * live_ref[...]
        sin = jnp.sin(ang) * sign_ref[...]

        def rope(y):
            return y * cos + pltpu.roll(y, half, 1) * sin

        k_rope = rope(k_rope * rk_rope * krg_ref[...]).astype(BF16)
        for hh in range(MLA_HEADS):
            sl = slice(hh * LANES, (hh + 1) * LANES)
            sl2 = slice(nope_all + hh * LANES, nope_all + (hh + 1) * LANES)
            q_ref[0, hh, rows, :LANES] = (q[:, sl] * rq_nope[hh] * qn_gain).astype(BF16)
            q_ref[0, hh, rows, LANES:] = rope(q[:, sl2] * rq_rope[hh] * qr_gain).astype(BF16)
            k_ref[0, hh, rows, :LANES] = (kv[:, sl] * rk_nope[hh] * kng_ref[...]).astype(BF16)
            k_ref[0, hh, rows, LANES:] = k_rope
            v_ref[0, hh, rows, :LANES] = kv[:, sl2].astype(BF16)
            v_ref[0, hh, rows, LANES:] = ones

    projected = project(0)
    for sub in range(MLA_SUBTILES):
        current = projected
        if sub + 1 < MLA_SUBTILES:
            projected = project(sub + 1)
        finish(sub, current)


def _mla_proj(h, posb, consts):
    bsz, seq, d = h.shape
    ts = TS_MLA
    out_sds = jax.ShapeDtypeStruct((bsz, MLA_HEADS, seq, QK_DIM_PADDED), BF16)
    out_spec = pl.BlockSpec((1, MLA_HEADS, ts, QK_DIM_PADDED), lambda b, s: (b, 0, s, 0))
    return pl.pallas_call(
        _mla_proj_kernel,
        out_shape=(out_sds, out_sds, out_sds),
        grid=(bsz, seq // ts),
        in_specs=[pl.BlockSpec((1, ts, d), lambda b, s: (b, s, 0)),
                  pl.BlockSpec((1, ts, LANES), lambda b, s: (b, s, 0))]
        + [_const_spec(c.shape) for c in consts],
        out_specs=(out_spec, out_spec, out_spec),
        compiler_params=pltpu.CompilerParams(
            dimension_semantics=("arbitrary", "arbitrary"), vmem_limit_bytes=VMEM_LIMIT),
        name="mla_proj",
    )(h, posb, *consts)


def _attn_kernel(q_ref, k_ref, v_ref, o_ref, m_ref, acc_ref, s0_ref, s1_ref):
    qi = pl.program_id(2)
    nhp, tq = q_ref.shape[1], q_ref.shape[2]
    tk = tq
    ng = nhp // 2
    m_ref[...] = jnp.full_like(m_ref, -jnp.inf)
    acc_ref[...] = jnp.zeros_like(acc_ref)

    def logits(kb, group, s_ref):
        start = pl.multiple_of(kb * tk, tk)
        for i in range(ng):
            hp = group * ng + i
            k = k_ref[0, hp, pl.ds(start, tk), :]
            s_ref[i] = lax.dot_general(q_ref[0, hp], k, (((1,), (1,)), ((), ())),
                                       preferred_element_type=F32)

    def softmax_pv(kb, group, s_ref, masked):
        start = pl.multiple_of(kb * tk, tk)
        for i in range(ng):
            hp = group * ng + i
            s = s_ref[i]
            if masked:
                row = lax.broadcasted_iota(jnp.int32, s.shape, 0)
                col = lax.broadcasted_iota(jnp.int32, s.shape, 1)
                s = jnp.where(col <= row, s, jnp.finfo(F32).min)
            m_prev = m_ref[hp]
            m_next = jnp.maximum(m_prev, jnp.max(s, axis=1, keepdims=True))
            alpha = jnp.exp2(m_prev - m_next)
            p = jnp.exp2(s - jnp.concatenate([m_next] * (tk // LANES), axis=1))
            v = v_ref[0, hp, pl.ds(start, tk), :]
            acc_ref[hp] = (jnp.concatenate([alpha, alpha], axis=1) * acc_ref[hp]
                           + _dot(p.astype(BF16), v))
            m_ref[hp] = m_next

    logits(0, 0, s0_ref)

    def body(kb, carry):
        logits(kb, 1, s1_ref)
        softmax_pv(kb, 0, s0_ref, False)
        logits(kb + 1, 0, s0_ref)
        softmax_pv(kb, 1, s1_ref, False)
        return carry

    lax.fori_loop(0, qi, body, 0)
    logits(qi, 1, s1_ref)
    softmax_pv(qi, 0, s0_ref, True)
    softmax_pv(qi, 1, s1_ref, True)

    for hp in range(nhp):
        acc = acc_ref[hp]
        o_ref[0, :, hp * LANES:(hp + 1) * LANES] = (acc[:, :LANES] / acc[:, LANES:]).astype(o_ref.dtype)


def _attention(q, k, v1):
    bsz, nh, seq, dq = q.shape
    tq = TQ_ATTN
    hp = ATTN_HEADS_PER_STEP
    return pl.pallas_call(
        _attn_kernel,
        out_shape=jax.ShapeDtypeStruct((bsz, seq, nh * V_HEAD_DIM), BF16),
        grid=(bsz, nh // hp, seq // tq),
        in_specs=[pl.BlockSpec((1, hp, tq, dq), lambda b, h, i: (b, h, i, 0)),
                  pl.BlockSpec((1, hp, seq, dq), lambda b, h, i: (b, h, 0, 0)),
                  pl.BlockSpec((1, hp, seq, dq), lambda b, h, i: (b, h, 0, 0))],
        out_specs=pl.BlockSpec((1, tq, hp * V_HEAD_DIM), lambda b, h, i: (b, i, h)),
        scratch_shapes=[pltpu.VMEM((hp, tq, LANES), F32), pltpu.VMEM((hp, tq, dq), F32),
                        pltpu.VMEM((hp // 2, tq, tq), F32), pltpu.VMEM((hp // 2, tq, tq), F32)],
        compiler_params=pltpu.CompilerParams(
            dimension_semantics=("arbitrary", "arbitrary", "arbitrary"),
            vmem_limit_bytes=VMEM_LIMIT),
        name="mla_attention",
    )(q, k, v1)


def _ffn_subtiles(load, store, g_ref, wgu_ref, wdn_ref):
    def prepare(i):
        h = load(i)
        return h, _rms(h, g_ref[...]).astype(BF16)

    prepared = prepare(0)
    for i in range(FFN_SUBTILES):
        h, xn = prepared
        out = h
        for ci, (lo, hi) in enumerate(FFN_CHUNK_BOUNDS):
            gate = _dot(xn, wgu_ref[:, lo:hi])
            up = _dot(xn, wgu_ref[:, D_FF + lo:D_FF + hi])
            if ci == 0 and i + 1 < FFN_SUBTILES:
                prepared = prepare(i + 1)
            act = (jax.nn.silu(gate) * up).astype(BF16)
            out = out + _dot(act, wdn_ref[lo:hi, :])
        store(i, out)


def _ffn_kernel(h_ref, g_ref, wgu_ref, wdn_ref, o_ref):
    rows = h_ref.shape[0] // FFN_SUBTILES

    def store(i, out):
        o_ref[i * rows:(i + 1) * rows] = out

    _ffn_subtiles(lambda i: h_ref[i * rows:(i + 1) * rows], store, g_ref, wgu_ref, wdn_ref)


def _ffn_from_time_major_kernel(h_ref, g_ref, wgu_ref, wdn_ref, o_ref):
    tt, nb, d = h_ref.shape
    ts = tt // FFN_SUBTILES

    def store(i, out):
        o_ref[:, i * ts:(i + 1) * ts, :] = jnp.swapaxes(out.reshape(ts, nb, d), 0, 1)

    _ffn_subtiles(lambda i: h_ref[i * ts:(i + 1) * ts].reshape(ts * nb, d), store,
                  g_ref, wgu_ref, wdn_ref)


def _proj_ffn_kernel(h_ref, a_ref, wo_ref, g_ref, wgu_ref, wdn_ref, o_ref):
    rows = h_ref.shape[0] // FFN_SUBTILES

    def load(i):
        sl = slice(i * rows, (i + 1) * rows)
        return h_ref[sl] + _dot(a_ref[sl], wo_ref[...])

    def store(i, out):
        o_ref[i * rows:(i + 1) * rows] = out

    _ffn_subtiles(load, store, g_ref, wgu_ref, wdn_ref)


def _ffn_from_time_major(h, g, w_gu, w_dn):
    seq, bsz, d = h.shape
    tt = TM_FFN // bsz
    return pl.pallas_call(
        _ffn_from_time_major_kernel,
        out_shape=jax.ShapeDtypeStruct((bsz, seq, d), h.dtype),
        grid=(seq // tt,),
        in_specs=[pl.BlockSpec((tt, bsz, d), lambda s: (s, 0, 0)),
                  _const_spec(g.shape), _const_spec(w_gu.shape), _const_spec(w_dn.shape)],
        out_specs=pl.BlockSpec((bsz, tt, d), lambda s: (0, s, 0)),
        compiler_params=pltpu.CompilerParams(
            dimension_semantics=("arbitrary",), vmem_limit_bytes=VMEM_LIMIT),
        name="ffn_tm",
    )(h, g, w_gu, w_dn)


def _ffn(h2, g, w_gu, w_dn):
    t, d = h2.shape
    tm = TM_FFN
    tok = pl.BlockSpec((tm, d), lambda i: (i, 0))
    return pl.pallas_call(
        _ffn_kernel,
        out_shape=jax.ShapeDtypeStruct(h2.shape, h2.dtype),
        grid=(t // tm,),
        in_specs=[tok, _const_spec(g.shape), _const_spec(w_gu.shape), _const_spec(w_dn.shape)],
        out_specs=tok,
        compiler_params=pltpu.CompilerParams(
            dimension_semantics=("arbitrary",), vmem_limit_bytes=VMEM_LIMIT),
        name="ffn",
    )(h2, g, w_gu, w_dn)


def _proj_ffn(h2, attn, w_o, g, w_gu, w_dn):
    t, d = h2.shape
    da = attn.shape[1]
    tm = TM_FFN
    tok = pl.BlockSpec((tm, d), lambda i: (i, 0))
    return pl.pallas_call(
        _proj_ffn_kernel,
        out_shape=jax.ShapeDtypeStruct((t, d), h2.dtype),
        grid=(t // tm,),
        in_specs=[tok,
                  pl.BlockSpec((tm, da), lambda i: (i, 0)),
                  _const_spec(w_o.shape), _const_spec(g.shape), _const_spec(w_gu.shape),
                  _const_spec(w_dn.shape)],
        out_specs=tok,
        compiler_params=pltpu.CompilerParams(
            dimension_semantics=("arbitrary",), vmem_limit_bytes=VMEM_LIMIT),
        name="proj_ffn",
    )(h2, attn, w_o, g, w_gu, w_dn)


def _row(v):
    return v.reshape(1, -1)


def _pad_lanes(v, n):
    return jnp.pad(v, [(0, 0)] * (v.ndim - 1) + [(0, n - v.shape[-1])])


def _dup(v):
    return jnp.concatenate([v, v], axis=-1)


def kernel(x, positions, mix_norm, conv_w_in, conv_w, conv_w_out, lru_w_in, lru_conv_w, lru_conv_b, lru_gate_a_w, lru_gate_a_b, lru_gate_x_w, lru_gate_x_b, lru_lambda, lru_w_out, mla_w_down, mla_q_norm, mla_kv_norm, mla_w_uq, mla_w_ukv, mla_qn_norm, mla_qr_norm, mla_kn_norm, mla_kr_norm, mla_w_o, ffn_norm, ffn_w_gu, ffn_w_down):
    bsz, seq, d = x.shape
    nh = MLA_HEADS
    half = QK_ROPE_DIM // 2

    inv_freq = ROPE_THETA ** (-jnp.arange(0, QK_ROPE_DIM, 2, dtype=F32) / QK_ROPE_DIM)
    freq_slab = _row(_pad_lanes(_dup(inv_freq), LANES))
    sign_slab = _row(_pad_lanes(jnp.concatenate([-jnp.ones(half, F32), jnp.ones(half, F32)]), LANES))
    live_slab = _row(_pad_lanes(jnp.ones(QK_ROPE_DIM, F32), LANES))
    lane_slab = jnp.arange(2 * LANES) // LANES
    pair_ones = ((lane_slab[:, None] == lane_slab[None, :]) * (1.0 / LANES)).astype(BF16)
    posb = jnp.broadcast_to(positions.astype(F32)[..., None], (bsz, seq, LANES))

    h = x
    for i in range(DEPTH):
        kind, j = i % N_MIXERS, i // N_MIXERS
        g = _row(mix_norm[i])
        gf, wgu, wdn = _row(ffn_norm[i]), ffn_w_gu[i].astype(BF16), ffn_w_down[i].astype(BF16)
        if kind == 0:
            conv_args = (h, g, conv_w_in[j].astype(BF16), conv_w[j], conv_w_out[j].astype(BF16))
            if i + 1 < DEPTH and (i + 1) % N_MIXERS == 1:
                h = _conv_layer_to_time_major(*conv_args)
                h = _ffn(h.reshape(seq * bsz, d), gf, wgu, wdn).reshape(seq, bsz, d)
            else:
                h = _conv_layer(*conv_args)
                h = _ffn(h.reshape(bsz * seq, d), gf, wgu, wdn).reshape(bsz, seq, d)
        elif kind == 1:
            wax = (0.5 * jnp.concatenate([lru_gate_a_w[j], lru_gate_x_w[j]], axis=-1)).astype(BF16)
            h = _lru_layer(h, g, lru_w_in[j].astype(BF16), lru_conv_w[j], _row(lru_conv_b[j]),
                           wax, _row(0.5 * lru_gate_a_b[j]), _row(0.5 * lru_gate_x_b[j]),
                           _row(lru_lambda[j]), lru_w_out[j].astype(BF16))
            h = _ffn_from_time_major(h, gf, wgu, wdn)
        else:
            wdm = mla_w_down[j]
            wd = jnp.concatenate([wdm, wdm[:, Q_LORA_RANK + KV_LORA_RANK:]], axis=1).astype(BF16)
            wq = mla_w_uq[j].reshape(Q_LORA_RANK, nh, QK_NOPE_DIM + QK_ROPE_DIM)
            wuq = jnp.concatenate(
                [wq[:, :, :QK_NOPE_DIM].reshape(Q_LORA_RANK, nh * QK_NOPE_DIM),
                 _dup(wq[:, :, QK_NOPE_DIM:]).reshape(Q_LORA_RANK, nh * LANES)],
                axis=1).astype(BF16)
            wkv = mla_w_ukv[j].reshape(KV_LORA_RANK, nh, QK_NOPE_DIM + V_HEAD_DIM)
            wukv = jnp.concatenate(
                [wkv[:, :, :QK_NOPE_DIM].reshape(KV_LORA_RANK, nh * QK_NOPE_DIM),
                 wkv[:, :, QK_NOPE_DIM:].reshape(KV_LORA_RANK, nh * V_HEAD_DIM)], axis=1).astype(BF16)
            consts = (g, wd, _row(mla_q_norm[j]), _row(mla_kv_norm[j]), wuq, wukv,
                      _row(mla_qn_norm[j]), _row(_dup(mla_qr_norm[j])),
                      _row(mla_kn_norm[j]), _row(_dup(mla_kr_norm[j])),
                      freq_slab, sign_slab, live_slab, pair_ones)
            q, k, v1 = _mla_proj(h, posb, consts)
            attn = _attention(q, k, v1).reshape(bsz * seq, nh * V_HEAD_DIM)
            h = _proj_ffn(h.reshape(bsz * seq, d), attn, mla_w_o[j].astype(BF16), gf, wgu,
                          wdn).reshape(bsz, seq, d)
    return h
```

```python
import math

import jax
import jax.numpy as jnp
from jax import lax
from jax.experimental import pallas as pl
from jax.experimental.pallas import tpu as pltpu

F32 = jnp.float32
BF16 = jnp.bfloat16

D_MODEL = 1024
DEPTH = 4
N_MIXERS = 3
CONV_WIDTH = 3
LRU_WIDTH = 1280
LRU_BLOCKS = 10
LRU_BLOCK_W = LRU_WIDTH // LRU_BLOCKS
LRU_CONV_WIDTH = 4
LRU_C = 8.0
MLA_HEADS = 8
Q_LORA_RANK = 384
KV_LORA_RANK = 256
QK_NOPE_DIM = 128
QK_ROPE_DIM = 64
V_HEAD_DIM = 128
ROPE_THETA = 10000.0
D_FF = 2816
NORM_EPS = 1e-6

LANES = 128
SUBLANES = 8
BF16_SUBLANES = 16
QK_DIM_PADDED = 2 * LANES
VMEM_LIMIT = 56 * 1024 * 1024

TS_CONV = 1024
CONV_SUBTILES = 2
TT_LRU = 128
LRU_SUBTILES = 4
TS_MLA = 1024
MLA_SUBTILES = 4
TQ_ATTN = 512
ATTN_HEADS_PER_STEP = 4
TM_FFN = 1024
FFN_SUBTILES = 2
MXU_DIM = 256
_FFN_SPLIT = (D_FF // MXU_DIM + 1) // 2 * MXU_DIM
FFN_CHUNK_BOUNDS = ((0, _FFN_SPLIT), (_FFN_SPLIT, D_FF))
LOG2E = 1.4426950408889634


def _rms(x, g):
    return x * lax.rsqrt(jnp.mean(x * x, axis=-1, keepdims=True) + NORM_EPS) * g


def _dot(a, b):
    return jnp.dot(a, b, preferred_element_type=F32)


def _gelu_tanh(x):
    k0 = -2.0 * LOG2E * math.sqrt(2.0 / math.pi)
    return x / (1.0 + jnp.exp2(x * (k0 + (k0 * 0.044715) * (x * x))))


def _cast_side_job(arrays, n_steps):
    specs, out_shapes = [], []
    for a in arrays:
        rows, cols = a.shape
        n_blocks = n_steps
        while rows % n_blocks or (rows // n_blocks) % BF16_SUBLANES:
            n_blocks //= 2
        rep = n_steps // n_blocks
        specs.append(pl.BlockSpec((rows // n_blocks, cols), lambda i, rep=rep: (i // rep, 0)))
        out_shapes.append(jax.ShapeDtypeStruct(a.shape, BF16))
    return specs, out_shapes


def _split_cast_refs(refs):
    n = (len(refs) - 1) // 2
    return refs[:n], refs[n], refs[n + 1:]


def _run_casts(srcs, dsts):
    for src, dst in zip(srcs, dsts):
        dst[...] = src[...].astype(dst.dtype)


def _const_spec(shape):
    nd = len(shape)
    return pl.BlockSpec(shape, lambda *_: (0,) * nd, pipeline_mode=pl.Buffered(1))


def _shift_rows_prev(x, prev8, d):
    r = pltpu.roll(x, d, 0)
    hd = pltpu.roll(prev8, d, 0)
    rows8 = lax.broadcasted_iota(jnp.int32, prev8.shape, 0)
    first = jnp.where(rows8 < d, hd, r[:SUBLANES])
    return jnp.concatenate([first, r[SUBLANES:]], axis=0)


def _conv_layer_kernel(h_ref, g_ref, win_ref, cw_ref, wout_ref, o_ref, carry_ref):
    @pl.when(pl.program_id(1) == 0)
    def _():
        carry_ref[...] = jnp.zeros_like(carry_ref)

    rows = h_ref.shape[1] // CONV_SUBTILES
    cw = cw_ref[...]

    def project_in(sub):
        h = h_ref[0, sub * rows:(sub + 1) * rows, :]
        return h, _dot(_rms(h, g_ref[...]).astype(BF16), win_ref[...])

    prev8 = carry_ref[...]
    projected = project_in(0)
    for sub in range(CONV_SUBTILES):
        h, bch = projected
        if sub + 1 < CONV_SUBTILES:
            projected = project_in(sub + 1)
        b_gate = bch[:, :D_MODEL]
        u = bch[:, D_MODEL:2 * D_MODEL] * bch[:, 2 * D_MODEL:]
        conv = (cw[0:1] * _shift_rows_prev(u, prev8, 2)
                + cw[1:2] * _shift_rows_prev(u, prev8, 1)
                + cw[2:3] * u)
        prev8 = u[rows - SUBLANES:]
        y = _dot((b_gate * conv).astype(BF16), wout_ref[...])
        o_ref[0, sub * rows:(sub + 1) * rows, :] = h + y
    carry_ref[...] = prev8


def _conv_layer(h, g, w_in, cw, w_out):
    bsz, seq, d = h.shape
    ts = TS_CONV
    tok = pl.BlockSpec((1, ts, d), lambda b, s: (b, s, 0))
    return pl.pallas_call(
        _conv_layer_kernel,
        out_shape=jax.ShapeDtypeStruct(h.shape, h.dtype),
        grid=(bsz, seq // ts),
        in_specs=[
            tok, _const_spec(g.shape), _const_spec(w_in.shape), _const_spec(cw.shape),
            _const_spec(w_out.shape),
        ],
        out_specs=tok,
        scratch_shapes=[pltpu.VMEM((SUBLANES, d), F32)],
        compiler_params=pltpu.CompilerParams(
            dimension_semantics=("arbitrary", "arbitrary"), vmem_limit_bytes=VMEM_LIMIT),
        name="conv_layer",
    )(h, g, w_in, cw, w_out)


def _conv_layer_tm_kernel(h_ref, g_ref, win_ref, cw_ref, wout_ref, *rest):
    cast_srcs, o_ref, cast_dsts = _split_cast_refs(rest[:-1])
    carry_ref = rest[-1]
    _run_casts(cast_srcs, cast_dsts)

    @pl.when(pl.program_id(0) == 0)
    def _():
        carry_ref[...] = jnp.zeros_like(carry_ref)

    nb, tt, d = h_ref.shape
    ts = tt // CONV_SUBTILES
    rows = ts * nb
    halo = (CONV_WIDTH - 1) * nb
    cw = cw_ref[...]

    def project_in(sub):
        h = jnp.swapaxes(h_ref[:, sub * ts:(sub + 1) * ts, :], 0, 1).reshape(rows, d)
        return h, _dot(_rms(h, g_ref[...]).astype(BF16), win_ref[...])

    u_prev = carry_ref[...]
    projected = project_in(0)
    for sub in range(CONV_SUBTILES):
        h, bch = projected
        if sub + 1 < CONV_SUBTILES:
            projected = project_in(sub + 1)
        b_gate = bch[:, :D_MODEL]
        u = bch[:, D_MODEL:2 * D_MODEL] * bch[:, 2 * D_MODEL:]
        ext = jnp.concatenate([u_prev, u], axis=0)
        u_prev = u[rows - halo:]
        conv = cw[CONV_WIDTH - 1:CONV_WIDTH] * u
        for k in range(CONV_WIDTH - 1):
            conv = conv + cw[k:k + 1] * ext[k * nb:k * nb + rows]
        y = _dot((b_gate * conv).astype(BF16), wout_ref[...])
        o_ref[sub * ts:(sub + 1) * ts] = (h + y).reshape(ts, nb, d)
    carry_ref[...] = u_prev


def _conv_layer_to_time_major(h, g, w_in, cw, w_out, cast=()):
    bsz, seq, d = h.shape
    assert bsz == SUBLANES
    tt = TS_CONV // bsz
    cast_specs, cast_shapes = _cast_side_job(cast, seq // tt)
    out = pl.pallas_call(
        _conv_layer_tm_kernel,
        out_shape=[jax.ShapeDtypeStruct((seq, bsz, d), h.dtype)] + cast_shapes,
        grid=(seq // tt,),
        in_specs=[
            pl.BlockSpec((bsz, tt, d), lambda s: (0, s, 0)),
            _const_spec(g.shape), _const_spec(w_in.shape), _const_spec(cw.shape),
            _const_spec(w_out.shape),
        ] + cast_specs,
        out_specs=[pl.BlockSpec((tt, bsz, d), lambda s: (s, 0, 0))] + cast_specs,
        scratch_shapes=[pltpu.VMEM(((CONV_WIDTH - 1) * bsz, d), F32)],
        compiler_params=pltpu.CompilerParams(
            dimension_semantics=("arbitrary",), vmem_limit_bytes=VMEM_LIMIT),
        name="conv_layer_tm",
    )(h, g, w_in, cw, w_out, *cast)
    return out[0], tuple(out[1:])


def _lru_layer_kernel(h_ref, g_ref, win_ref, cw_ref, cb_ref, wax_ref, ba_ref, bx_ref, lam_ref,
                      wout_ref, o_ref, xcarry_ref, hcarry_ref):
    @pl.when(pl.program_id(0) == 0)
    def _():
        xcarry_ref[...] = jnp.zeros_like(xcarry_ref)
        hcarry_ref[...] = jnp.zeros_like(hcarry_ref)

    tt, nb, d = h_ref.shape
    ts = tt // LRU_SUBTILES
    rows = ts * nb
    halo = (LRU_CONV_WIDTH - 1) * nb
    cw = cw_ref[...]
    lam = lam_ref[...]
    log_sig_lam = jnp.minimum(lam, 0.0) - jnp.log1p(jnp.exp(-jnp.abs(lam)))
    neg_half_c_lsl = (-0.5 * LRU_C) * log_sig_lam
    ba = ba_ref[...]
    bx = bx_ref[...]
    x_prev = xcarry_ref[...]
    h_prev = hcarry_ref[...]

    def project_in(sub):
        h = h_ref[sub * ts:(sub + 1) * ts].reshape(rows, d)
        xn = _rms(h, g_ref[...]).astype(BF16)
        return h, _dot(xn, win_ref[...])

    projected = project_in(0)
    for sub in range(LRU_SUBTILES):
        t0 = sub * ts
        h, gr = projected
        if sub + 1 < LRU_SUBTILES:
            projected = project_in(sub + 1)
        gate = _gelu_tanh(gr[:, :LRU_WIDTH])
        xr = gr[:, LRU_WIDTH:]
        ext = jnp.concatenate([x_prev, xr], axis=0)
        x_prev = xr[rows - halo:]
        rec = cb_ref[...] + cw[LRU_CONV_WIDTH - 1:LRU_CONV_WIDTH] * xr
        for k in range(LRU_CONV_WIDTH - 1):
            rec = rec + cw[k:k + 1] * ext[k * nb:k * nb + rows]

        outs = []
        for n in range(LRU_BLOCKS):
            sl = slice(n * LRU_BLOCK_W, (n + 1) * LRU_BLOCK_W)
            xb = rec[:, sl]
            ri = _dot(xb.astype(BF16), wax_ref[n])
            neg_log_a = neg_half_c_lsl[:, sl] * (jnp.tanh(ri[:, :LRU_BLOCK_W] + ba[:, sl]) + 1.0)
            i = 0.5 * jnp.tanh(ri[:, LRU_BLOCK_W:] + bx[:, sl]) + 0.5
            a = jnp.exp2(neg_log_a * (-LOG2E))
            z = jnp.tanh(neg_log_a) * (a * a + 1.0)
            mult = jnp.where(z > 0.0, z * lax.rsqrt(z), 0.0)
            b = mult * (i * xb)
            hp = h_prev[:, sl]
            steps = []
            for t in range(ts):
                hp = a[t * nb:(t + 1) * nb] * hp + b[t * nb:(t + 1) * nb]
                steps.append(hp)
            outs.append(jnp.concatenate(steps, axis=0))
        hs = jnp.concatenate(outs, axis=1)
        h_prev = hs[rows - nb:]
        y = _dot((gate * hs).astype(BF16), wout_ref[...])
        o_ref[t0:t0 + ts] = (h + y).reshape(ts, nb, d)

    xcarry_ref[...] = x_prev
    hcarry_ref[...] = h_prev


def _lru_layer(h, g, w_in, cw, cb, wax, ba, bx, lam, w_out):
    seq, bsz, d = h.shape
    assert bsz == SUBLANES
    tt = TT_LRU
    consts = (g, w_in, cw, cb, wax, ba, bx, lam, w_out)
    tok = pl.BlockSpec((tt, bsz, d), lambda s: (s, 0, 0))
    return pl.pallas_call(
        _lru_layer_kernel,
        out_shape=jax.ShapeDtypeStruct(h.shape, h.dtype),
        grid=(seq // tt,),
        in_specs=[tok] + [_const_spec(c.shape) for c in consts],
        out_specs=tok,
        scratch_shapes=[pltpu.VMEM(((LRU_CONV_WIDTH - 1) * bsz, LRU_WIDTH), F32),
                        pltpu.VMEM((bsz, LRU_WIDTH), F32)],
        compiler_params=pltpu.CompilerParams(
            dimension_semantics=("arbitrary",), vmem_limit_bytes=VMEM_LIMIT),
        name="lru_layer",
    )(h, *consts)


def _mla_proj_kernel(h_ref, pos_ref, g_ref, wd_ref, qg_ref, kvg_ref, wuq_ref, wukv_ref,
                     qng_ref, qrg_ref, kng_ref, krg_ref, freq_ref, sign_ref, live_ref, pair_ref,
                     q_ref, k_ref, v_ref):
    ts = h_ref.shape[1] // MLA_SUBTILES
    nope_all = MLA_HEADS * QK_NOPE_DIM
    half = QK_ROPE_DIM // 2
    q_scale = LOG2E / math.sqrt(QK_NOPE_DIM + QK_ROPE_DIM)
    qn_gain = qng_ref[...] * q_scale
    qr_gain = qrg_ref[...] * q_scale
    ones = jnp.ones((ts, LANES), BF16)

    def slab_norm_scales(x):
        scales = []
        for p in range(x.shape[1] // (2 * LANES)):
            xs = x[:, p * 2 * LANES:(p + 1) * 2 * LANES]
            r = lax.rsqrt(_dot((xs * xs).astype(BF16), pair_ref[...]) + NORM_EPS)
            scales += [r[:, :LANES], r[:, LANES:]]
        return scales

    def project(sub):
        rows = pl.ds(sub * ts, ts)
        xn = _rms(h_ref[0, rows, :], g_ref[...]).astype(BF16)
        c = _dot(xn, wd_ref[...])
        c_q = c[:, :Q_LORA_RANK]
        c_kv = c[:, Q_LORA_RANK:Q_LORA_RANK + KV_LORA_RANK]
        k_rope = c[:, Q_LORA_RANK + KV_LORA_RANK:]
        q = _dot(_rms(c_q, qg_ref[...]).astype(BF16), wuq_ref[...])
        kv = _dot(_rms(c_kv, kvg_ref[...]).astype(BF16), wukv_ref[...])
        rq_nope = slab_norm_scales(q[:, :nope_all])
        rq_rope = slab_norm_scales(q[:, nope_all:])
        rk_nope = slab_norm_scales(kv[:, :nope_all])
        rk_rope = lax.rsqrt(_dot((k_rope * k_rope).astype(BF16), pair_ref[:LANES, :LANES]) + NORM_EPS)
        return q, kv, k_rope, rq_nope, rq_rope, rk_nope, rk_rope

    def finish(sub, projected):
        q, kv, k_rope, rq_nope, rq_rope, rk_nope, rk_rope = projected
        rows = pl.ds(sub * ts, ts)
        ang = pos_ref[0, rows, :] * freq_ref[...]
        cos = jnp.cos(ang) * live_ref[...]
        sin = jnp.sin(ang) * sign_ref[...]

        def rope(y):
            return y * cos + pltpu.roll(y, half, 1) * sin

        k_rope = rope(k_rope * rk_rope * krg_ref[...]).astype(BF16)
        for hh in range(MLA_HEADS):
            sl = slice(hh * LANES, (hh + 1) * LANES)
            sl2 = slice(nope_all + hh * LANES, nope_all + (hh + 1) * LANES)
            q_ref[0, hh, rows, :LANES] = (q[:, sl] * rq_nope[hh] * qn_gain).astype(BF16)
            q_ref[0, hh, rows, LANES:] = rope(q[:, sl2] * rq_rope[hh] * qr_gain).astype(BF16)
            k_ref[0, hh, rows, :LANES] = (kv[:, sl] * rk_nope[hh] * kng_ref[...]).astype(BF16)
            k_ref[0, hh, rows, LANES:] = k_rope
            v_ref[0, hh, rows, :LANES] = kv[:, sl2].astype(BF16)
            v_ref[0, hh, rows, LANES:] = ones

    projected = project(0)
    for sub in range(MLA_SUBTILES):
        current = projected
        if sub + 1 < MLA_SUBTILES:
            projected = project(sub + 1)
        finish(sub, current)


def _mla_proj(h, posb, consts):
    bsz, seq, d = h.shape
    ts = TS_MLA
    out_sds = jax.ShapeDtypeStruct((bsz, MLA_HEADS, seq, QK_DIM_PADDED), BF16)
    out_spec = pl.BlockSpec((1, MLA_HEADS, ts, QK_DIM_PADDED), lambda b, s: (b, 0, s, 0))
    return pl.pallas_call(
        _mla_proj_kernel,
        out_shape=(out_sds, out_sds, out_sds),
        grid=(bsz, seq // ts),
        in_specs=[pl.BlockSpec((1, ts, d), lambda b, s: (b, s, 0)),
                  pl.BlockSpec((1, ts, LANES), lambda b, s: (b, s, 0))]
        + [_const_spec(c.shape) for c in consts],
        out_specs=(out_spec, out_spec, out_spec),
        compiler_params=pltpu.CompilerParams(
            dimension_semantics=("arbitrary", "arbitrary"), vmem_limit_bytes=VMEM_LIMIT),
        name="mla_proj",
    )(h, posb, *consts)


def _attn_kernel(q_ref, k_ref, v_ref, o_ref, m_ref, acc_ref, s0_ref, s1_ref):
    qi = pl.program_id(2)
    nhp, tq = q_ref.shape[1], q_ref.shape[2]
    tk = tq
    ng = nhp // 2
    m_ref[...] = jnp.full_like(m_ref, -jnp.inf)
    acc_ref[...] = jnp.zeros_like(acc_ref)

    def logits(kb, group, s_ref):
        start = pl.multiple_of(kb * tk, tk)
        for i in range(ng):
            hp = group * ng + i
            k = k_ref[0, hp, pl.ds(start, tk), :]
            s_ref[i] = lax.dot_general(q_ref[0, hp], k, (((1,), (1,)), ((), ())),
                                       preferred_element_type=F32)

    def softmax_pv(kb, group, s_ref, masked):
        start = pl.multiple_of(kb * tk, tk)
        for i in range(ng):
            hp = group * ng + i
            s = s_ref[i]
            if masked:
                row = lax.broadcasted_iota(jnp.int32, s.shape, 0)
                col = lax.broadcasted_iota(jnp.int32, s.shape, 1)
                s = jnp.where(col <= row, s, jnp.finfo(F32).min)
            m_prev = m_ref[hp]
            m_next = jnp.maximum(m_prev, jnp.max(s, axis=1, keepdims=True))
            alpha = jnp.exp2(m_prev - m_next)
            p = jnp.exp2(s - jnp.concatenate([m_next] * (tk // LANES), axis=1))
            v = v_ref[0, hp, pl.ds(start, tk), :]
            acc_ref[hp] = (jnp.concatenate([alpha, alpha], axis=1) * acc_ref[hp]
                           + _dot(p.astype(BF16), v))
            m_ref[hp] = m_next

    logits(0, 0, s0_ref)

    def unmasked_block(kb):
        logits(kb, 1, s1_ref)
        softmax_pv(kb, 0, s0_ref, False)
        logits(kb + 1, 0, s0_ref)
        softmax_pv(kb, 1, s1_ref, False)

    odd = qi % 2

    @pl.when(odd == 1)
    def _():
        unmasked_block(0)

    def body(j, carry):
        unmasked_block(odd + 2 * j)
        unmasked_block(odd + 2 * j + 1)
        return carry

    lax.fori_loop(0, qi // 2, body, 0)
    logits(qi, 1, s1_ref)
    softmax_pv(qi, 0, s0_ref, True)
    softmax_pv(qi, 1, s1_ref, True)

    for hp in range(nhp):
        acc = acc_ref[hp]
        o_ref[0, :, hp * LANES:(hp + 1) * LANES] = (acc[:, :LANES] / acc[:, LANES:]).astype(o_ref.dtype)


def _attention(q, k, v1):
    bsz, nh, seq, dq = q.shape
    tq = TQ_ATTN
    hp = ATTN_HEADS_PER_STEP
    return pl.pallas_call(
        _attn_kernel,
        out_shape=jax.ShapeDtypeStruct((bsz, seq, nh * V_HEAD_DIM), BF16),
        grid=(bsz, nh // hp, seq // tq),
        in_specs=[pl.BlockSpec((1, hp, tq, dq), lambda b, h, i: (b, h, i, 0)),
                  pl.BlockSpec((1, hp, seq, dq), lambda b, h, i: (b, h, 0, 0)),
                  pl.BlockSpec((1, hp, seq, dq), lambda b, h, i: (b, h, 0, 0))],
        out_specs=pl.BlockSpec((1, tq, hp * V_HEAD_DIM), lambda b, h, i: (b, i, h)),
        scratch_shapes=[pltpu.VMEM((hp, tq, LANES), F32), pltpu.VMEM((hp, tq, dq), F32),
                        pltpu.VMEM((hp // 2, tq, tq), F32), pltpu.VMEM((hp // 2, tq, tq), F32)],
        compiler_params=pltpu.CompilerParams(
            dimension_semantics=("arbitrary", "arbitrary", "arbitrary"),
            vmem_limit_bytes=VMEM_LIMIT),
        name="mla_attention",
    )(q, k, v1)


def _ffn_subtiles(load, store, g_ref, wgu_ref, wdn_ref):
    def prepare(i):
        h = load(i)
        return h, _rms(h, g_ref[...]).astype(BF16)

    prepared = prepare(0)
    for i in range(FFN_SUBTILES):
        h, xn = prepared
        out = h
        for ci, (lo, hi) in enumerate(FFN_CHUNK_BOUNDS):
            gate = _dot(xn, wgu_ref[:, lo:hi])
            up = _dot(xn, wgu_ref[:, D_FF + lo:D_FF + hi])
            if ci == 0 and i + 1 < FFN_SUBTILES:
                prepared = prepare(i + 1)
            act = (jax.nn.silu(gate) * up).astype(BF16)
            out = out + _dot(act, wdn_ref[lo:hi, :])
        store(i, out)


def _ffn_kernel(h_ref, g_ref, wgu_ref, wdn_ref, *rest):
    cast_srcs, o_ref, cast_dsts = _split_cast_refs(rest)
    _run_casts(cast_srcs, cast_dsts)
    rows = h_ref.shape[0] // FFN_SUBTILES

    def store(i, out):
        o_ref[i * rows:(i + 1) * rows] = out

    _ffn_subtiles(lambda i: h_ref[i * rows:(i + 1) * rows], store, g_ref, wgu_ref, wdn_ref)


def _ffn_from_time_major_kernel(h_ref, g_ref, wgu_ref, wdn_ref, *rest):
    cast_srcs, o_ref, cast_dsts = _split_cast_refs(rest)
    _run_casts(cast_srcs, cast_dsts)
    tt, nb, d = h_ref.shape
    ts = tt // FFN_SUBTILES

    def store(i, out):
        o_ref[:, i * ts:(i + 1) * ts, :] = jnp.swapaxes(out.reshape(ts, nb, d), 0, 1)

    _ffn_subtiles(lambda i: h_ref[i * ts:(i + 1) * ts].reshape(ts * nb, d), store,
                  g_ref, wgu_ref, wdn_ref)


def _proj_ffn_kernel(h_ref, a_ref, wo_ref, g_ref, wgu_ref, wdn_ref, *rest):
    cast_srcs, o_ref, cast_dsts = _split_cast_refs(rest)
    _run_casts(cast_srcs, cast_dsts)
    rows = h_ref.shape[0] // FFN_SUBTILES

    def load(i):
        sl = slice(i * rows, (i + 1) * rows)
        return h_ref[sl] + _dot(a_ref[sl], wo_ref[...])

    def store(i, out):
        o_ref[i * rows:(i + 1) * rows] = out

    _ffn_subtiles(load, store, g_ref, wgu_ref, wdn_ref)


def _ffn_call(kernel, name, n_steps, tokens, token_specs, consts, out_sds, out_spec, cast):
    cast_specs, cast_shapes = _cast_side_job(cast, n_steps)
    out = pl.pallas_call(
        kernel,
        out_shape=[out_sds] + cast_shapes,
        grid=(n_steps,),
        in_specs=list(token_specs) + [_const_spec(c.shape) for c in consts] + cast_specs,
        out_specs=[out_spec] + cast_specs,
        compiler_params=pltpu.CompilerParams(
            dimension_semantics=("arbitrary",), vmem_limit_bytes=VMEM_LIMIT),
        name=name,
    )(*tokens, *consts, *cast)
    return out[0], tuple(out[1:])


def _ffn_from_time_major(h, g, w_gu, w_dn, cast=()):
    seq, bsz, d = h.shape
    tt = TM_FFN // bsz
    return _ffn_call(_ffn_from_time_major_kernel, "ffn_tm", seq // tt, (h,),
                     [pl.BlockSpec((tt, bsz, d), lambda s: (s, 0, 0))], (g, w_gu, w_dn),
                     jax.ShapeDtypeStruct((bsz, seq, d), h.dtype),
                     pl.BlockSpec((bsz, tt, d), lambda s: (0, s, 0)), cast)


def _ffn(h2, g, w_gu, w_dn, cast=()):
    t, d = h2.shape
    tok = pl.BlockSpec((TM_FFN, d), lambda i: (i, 0))
    return _ffn_call(_ffn_kernel, "ffn", t // TM_FFN, (h2,), [tok], (g, w_gu, w_dn),
                     jax.ShapeDtypeStruct(h2.shape, h2.dtype), tok, cast)


def _proj_ffn(h2, attn, w_o, g, w_gu, w_dn, cast=()):
    t, d = h2.shape
    tok = pl.BlockSpec((TM_FFN, d), lambda i: (i, 0))
    return _ffn_call(_proj_ffn_kernel, "proj_ffn", t // TM_FFN, (h2, attn),
                     [tok, pl.BlockSpec((TM_FFN, attn.shape[1]), lambda i: (i, 0))],
                     (w_o, g, w_gu, w_dn), jax.ShapeDtypeStruct((t, d), h2.dtype), tok, cast)


def _row(v):
    return v.reshape(1, -1)


def _pad_lanes(v, n):
    return jnp.pad(v, [(0, 0)] * (v.ndim - 1) + [(0, n - v.shape[-1])])


def _dup(v):
    return jnp.concatenate([v, v], axis=-1)


def kernel(x, positions, mix_norm, conv_w_in, conv_w, conv_w_out, lru_w_in, lru_conv_w, lru_conv_b, lru_gate_a_w, lru_gate_a_b, lru_gate_x_w, lru_gate_x_b, lru_lambda, lru_w_out, mla_w_down, mla_q_norm, mla_kv_norm, mla_w_uq, mla_w_ukv, mla_qn_norm, mla_qr_norm, mla_kn_norm, mla_kr_norm, mla_w_o, ffn_norm, ffn_w_gu, ffn_w_down):
    bsz, seq, d = x.shape
    nh = MLA_HEADS
    half = QK_ROPE_DIM // 2

    inv_freq = ROPE_THETA ** (-jnp.arange(0, QK_ROPE_DIM, 2, dtype=F32) / QK_ROPE_DIM)
    freq_slab = _row(_pad_lanes(_dup(inv_freq), LANES))
    sign_slab = _row(_pad_lanes(jnp.concatenate([-jnp.ones(half, F32), jnp.ones(half, F32)]), LANES))
    live_slab = _row(_pad_lanes(jnp.ones(QK_ROPE_DIM, F32), LANES))
    lane_slab = jnp.arange(2 * LANES) // LANES
    pair_ones = ((lane_slab[:, None] == lane_slab[None, :]) * (1.0 / LANES)).astype(BF16)
    posb = jnp.broadcast_to(positions.astype(F32)[..., None], (bsz, seq, LANES))

    h = x
    ffn_w = None
    for i in range(DEPTH):
        kind, j = i % N_MIXERS, i // N_MIXERS
        g = _row(mix_norm[i])
        gf = _row(ffn_norm[i])
        to_time_major = kind == 0 and i + 1 < DEPTH and (i + 1) % N_MIXERS == 1
        if ffn_w is None and not to_time_major:
            ffn_w = (ffn_w_gu[i].astype(BF16), ffn_w_down[i].astype(BF16))
        cast_next = (ffn_w_gu[i + 1], ffn_w_down[i + 1]) if i + 1 < DEPTH else ()
        if kind == 0:
            conv_args = (h, g, conv_w_in[j].astype(BF16), conv_w[j], conv_w_out[j].astype(BF16))
            if to_time_major:
                h, cast_now = _conv_layer_to_time_major(
                    *conv_args, cast=() if ffn_w else (ffn_w_gu[i], ffn_w_down[i]))
                h, ffn_w = _ffn(h.reshape(seq * bsz, d), gf, *(ffn_w or cast_now), cast=cast_next)
                h = h.reshape(seq, bsz, d)
            else:
                h = _conv_layer(*conv_args)
                h, ffn_w = _ffn(h.reshape(bsz * seq, d), gf, *ffn_w, cast=cast_next)
                h = h.reshape(bsz, seq, d)
        elif kind == 1:
            wax = (0.5 * jnp.concatenate([lru_gate_a_w[j], lru_gate_x_w[j]], axis=-1)).astype(BF16)
            h = _lru_layer(h, g, lru_w_in[j].astype(BF16), lru_conv_w[j], _row(lru_conv_b[j]),
                           wax, _row(0.5 * lru_gate_a_b[j]), _row(0.5 * lru_gate_x_b[j]),
                           _row(lru_lambda[j]), lru_w_out[j].astype(BF16))
            h, ffn_w = _ffn_from_time_major(h, gf, *ffn_w, cast=cast_next)
        else:
            wdm = mla_w_down[j]
            wd = jnp.concatenate([wdm, wdm[:, Q_LORA_RANK + KV_LORA_RANK:]], axis=1).astype(BF16)
            wq = mla_w_uq[j].reshape(Q_LORA_RANK, nh, QK_NOPE_DIM + QK_ROPE_DIM)
            wuq = jnp.concatenate(
                [wq[:, :, :QK_NOPE_DIM].reshape(Q_LORA_RANK, nh * QK_NOPE_DIM),
                 _dup(wq[:, :, QK_NOPE_DIM:]).reshape(Q_LORA_RANK, nh * LANES)],
                axis=1).astype(BF16)
            wkv = mla_w_ukv[j].reshape(KV_LORA_RANK, nh, QK_NOPE_DIM + V_HEAD_DIM)
            wukv = jnp.concatenate(
                [wkv[:, :, :QK_NOPE_DIM].reshape(KV_LORA_RANK, nh * QK_NOPE_DIM),
                 wkv[:, :, QK_NOPE_DIM:].reshape(KV_LORA_RANK, nh * V_HEAD_DIM)], axis=1).astype(BF16)
            consts = (g, wd, _row(mla_q_norm[j]), _row(mla_kv_norm[j]), wuq, wukv,
                      _row(mla_qn_norm[j]), _row(_dup(mla_qr_norm[j])),
                      _row(mla_kn_norm[j]), _row(_dup(mla_kr_norm[j])),
                      freq_slab, sign_slab, live_slab, pair_ones)
            q, k, v1 = _mla_proj(h, posb, consts)
            attn = _attention(q, k, v1).reshape(bsz * seq, nh * V_HEAD_DIM)
            h, ffn_w = _proj_ffn(h.reshape(bsz * seq, d), attn, mla_w_o[j].astype(BF16), gf,
                                 *ffn_w, cast=cast_next)
            h = h.reshape(bsz, seq, d)
        ffn_w = ffn_w or None
    return h
```

```python
import math

import jax
import jax.numpy as jnp
from jax import lax
from jax.experimental import pallas as pl
from jax.experimental.pallas import tpu as pltpu

F32 = jnp.float32
BF16 = jnp.bfloat16

D_MODEL = 1024
DEPTH = 4
N_MIXERS = 3
CONV_WIDTH = 3
LRU_WIDTH = 1280
LRU_BLOCKS = 10
LRU_BLOCK_W = LRU_WIDTH // LRU_BLOCKS
LRU_CONV_WIDTH = 4
LRU_C = 8.0
MLA_HEADS = 8
Q_LORA_RANK = 384
KV_LORA_RANK = 256
QK_NOPE_DIM = 128
QK_ROPE_DIM = 64
V_HEAD_DIM = 128
ROPE_THETA = 10000.0
D_FF = 2816
NORM_EPS = 1e-6

LANES = 128
SUBLANES = 8
BF16_SUBLANES = 16
QK_DIM_PADDED = 2 * LANES
VMEM_LIMIT = 56 * 1024 * 1024

TS_CONV = 1024
CONV_SUBTILES = 2
TT_LRU = 128
LRU_SUBTILES = 4
TS_MLA = 1024
MLA_SUBTILES = 4
TQ_ATTN = 512
ATTN_HEADS_PER_STEP = 4
TM_FFN = 1024
FFN_SUBTILES = 2
MXU_DIM = 256
_FFN_SPLIT = (D_FF // MXU_DIM + 1) // 2 * MXU_DIM
FFN_CHUNK_BOUNDS = ((0, _FFN_SPLIT), (_FFN_SPLIT, D_FF))
LOG2E = 1.4426950408889634


def _rms(x, g):
    return x * lax.rsqrt(jnp.mean(x * x, axis=-1, keepdims=True) + NORM_EPS) * g


def _dot(a, b):
    return jnp.dot(a, b, preferred_element_type=F32)


def _gelu_tanh(x):
    k0 = -2.0 * LOG2E * math.sqrt(2.0 / math.pi)
    return x / (1.0 + jnp.exp2(x * (k0 + (k0 * 0.044715) * (x * x))))


def _cast_side_job(jobs, n_steps):
    in_specs, out_specs, out_shapes = [], [], []
    for stacked, index in jobs:
        _, rows, cols = stacked.shape
        n_blocks = n_steps
        while rows % n_blocks or (rows // n_blocks) % BF16_SUBLANES:
            n_blocks //= 2
        rep = n_steps // n_blocks
        in_specs.append(pl.BlockSpec((None, rows // n_blocks, cols),
                                     lambda i, rep=rep, index=index: (index, i // rep, 0)))
        out_specs.append(pl.BlockSpec((rows // n_blocks, cols), lambda i, rep=rep: (i // rep, 0)))
        out_shapes.append(jax.ShapeDtypeStruct((rows, cols), BF16))
    return in_specs, out_specs, out_shapes


def _split_cast_refs(refs):
    n = (len(refs) - 1) // 2
    return refs[:n], refs[n], refs[n + 1:]


def _run_casts(srcs, dsts):
    for src, dst in zip(srcs, dsts):
        dst[...] = src[...].astype(dst.dtype)


def _const_spec(shape):
    nd = len(shape)
    return pl.BlockSpec(shape, lambda *_: (0,) * nd, pipeline_mode=pl.Buffered(1))


def _shift_rows_prev(x, prev8, d):
    r = pltpu.roll(x, d, 0)
    hd = pltpu.roll(prev8, d, 0)
    rows8 = lax.broadcasted_iota(jnp.int32, prev8.shape, 0)
    first = jnp.where(rows8 < d, hd, r[:SUBLANES])
    return jnp.concatenate([first, r[SUBLANES:]], axis=0)


def _conv_layer_kernel(h_ref, g_ref, win_ref, cw_ref, wout_ref, o_ref, carry_ref):
    @pl.when(pl.program_id(1) == 0)
    def _():
        carry_ref[...] = jnp.zeros_like(carry_ref)

    rows = h_ref.shape[1] // CONV_SUBTILES
    cw = cw_ref[...]

    def project_in(sub):
        h = h_ref[0, sub * rows:(sub + 1) * rows, :]
        return h, _dot(_rms(h, g_ref[...]).astype(BF16), win_ref[...])

    prev8 = carry_ref[...]
    projected = project_in(0)
    for sub in range(CONV_SUBTILES):
        h, bch = projected
        if sub + 1 < CONV_SUBTILES:
            projected = project_in(sub + 1)
        b_gate = bch[:, :D_MODEL]
        u = bch[:, D_MODEL:2 * D_MODEL] * bch[:, 2 * D_MODEL:]
        conv = (cw[0:1] * _shift_rows_prev(u, prev8, 2)
                + cw[1:2] * _shift_rows_prev(u, prev8, 1)
                + cw[2:3] * u)
        prev8 = u[rows - SUBLANES:]
        y = _dot((b_gate * conv).astype(BF16), wout_ref[...])
        o_ref[0, sub * rows:(sub + 1) * rows, :] = h + y
    carry_ref[...] = prev8


def _conv_layer(h, g, w_in, cw, w_out):
    bsz, seq, d = h.shape
    ts = TS_CONV
    tok = pl.BlockSpec((1, ts, d), lambda b, s: (b, s, 0))
    return pl.pallas_call(
        _conv_layer_kernel,
        out_shape=jax.ShapeDtypeStruct(h.shape, h.dtype),
        grid=(bsz, seq // ts),
        in_specs=[
            tok, _const_spec(g.shape), _const_spec(w_in.shape), _const_spec(cw.shape),
            _const_spec(w_out.shape),
        ],
        out_specs=tok,
        scratch_shapes=[pltpu.VMEM((SUBLANES, d), F32)],
        compiler_params=pltpu.CompilerParams(
            dimension_semantics=("arbitrary", "arbitrary"), vmem_limit_bytes=VMEM_LIMIT),
        name="conv_layer",
    )(h, g, w_in, cw, w_out)


def _conv_layer_tm_kernel(h_ref, g_ref, win_ref, cw_ref, wout_ref, *rest):
    cast_srcs, o_ref, cast_dsts = _split_cast_refs(rest[:-1])
    carry_ref = rest[-1]
    _run_casts(cast_srcs, cast_dsts)

    @pl.when(pl.program_id(0) == 0)
    def _():
        carry_ref[...] = jnp.zeros_like(carry_ref)

    nb, tt, d = h_ref.shape
    ts = tt // CONV_SUBTILES
    rows = ts * nb
    halo = (CONV_WIDTH - 1) * nb
    cw = cw_ref[...]

    def project_in(sub):
        h = jnp.swapaxes(h_ref[:, sub * ts:(sub + 1) * ts, :], 0, 1).reshape(rows, d)
        return h, _dot(_rms(h, g_ref[...]).astype(BF16), win_ref[...])

    u_prev = carry_ref[...]
    projected = project_in(0)
    for sub in range(CONV_SUBTILES):
        h, bch = projected
        if sub + 1 < CONV_SUBTILES:
            projected = project_in(sub + 1)
        b_gate = bch[:, :D_MODEL]
        u = bch[:, D_MODEL:2 * D_MODEL] * bch[:, 2 * D_MODEL:]
        ext = jnp.concatenate([u_prev, u], axis=0)
        u_prev = u[rows - halo:]
        conv = cw[CONV_WIDTH - 1:CONV_WIDTH] * u
        for k in range(CONV_WIDTH - 1):
            conv = conv + cw[k:k + 1] * ext[k * nb:k * nb + rows]
        y = _dot((b_gate * conv).astype(BF16), wout_ref[...])
        o_ref[sub * ts:(sub + 1) * ts] = (h + y).reshape(ts, nb, d)
    carry_ref[...] = u_prev


def _conv_layer_to_time_major(h, g, w_in, cw, w_out, cast=()):
    bsz, seq, d = h.shape
    assert bsz == SUBLANES
    tt = TS_CONV // bsz
    cast_in, cast_out, cast_shapes = _cast_side_job(cast, seq // tt)
    out = pl.pallas_call(
        _conv_layer_tm_kernel,
        out_shape=[jax.ShapeDtypeStruct((seq, bsz, d), h.dtype)] + cast_shapes,
        grid=(seq // tt,),
        in_specs=[
            pl.BlockSpec((bsz, tt, d), lambda s: (0, s, 0)),
            _const_spec(g.shape), _const_spec(w_in.shape), _const_spec(cw.shape),
            _const_spec(w_out.shape),
        ] + cast_in,
        out_specs=[pl.BlockSpec((tt, bsz, d), lambda s: (s, 0, 0))] + cast_out,
        scratch_shapes=[pltpu.VMEM(((CONV_WIDTH - 1) * bsz, d), F32)],
        compiler_params=pltpu.CompilerParams(
            dimension_semantics=("arbitrary",), vmem_limit_bytes=VMEM_LIMIT),
        name="conv_layer_tm",
    )(h, g, w_in, cw, w_out, *[stacked for stacked, _ in cast])
    return out[0], tuple(out[1:])


def _lru_layer_kernel(h_ref, g_ref, win_ref, cw_ref, cb_ref, wax_ref, ba_ref, bx_ref, lam_ref,
                      wout_ref, o_ref, xcarry_ref, hcarry_ref):
    @pl.when(pl.program_id(0) == 0)
    def _():
        xcarry_ref[...] = jnp.zeros_like(xcarry_ref)
        hcarry_ref[...] = jnp.zeros_like(hcarry_ref)

    tt, nb, d = h_ref.shape
    ts = tt // LRU_SUBTILES
    rows = ts * nb
    halo = (LRU_CONV_WIDTH - 1) * nb
    cw = cw_ref[...]
    lam = lam_ref[...]
    log_sig_lam = jnp.minimum(lam, 0.0) - jnp.log1p(jnp.exp(-jnp.abs(lam)))
    neg_half_c_lsl = (-0.5 * LRU_C) * log_sig_lam
    ba = ba_ref[...]
    bx = bx_ref[...]
    x_prev = xcarry_ref[...]
    h_prev = hcarry_ref[...]

    def project_in(sub):
        h = h_ref[sub * ts:(sub + 1) * ts].reshape(rows, d)
        xn = _rms(h, g_ref[...]).astype(BF16)
        return h, _dot(xn, win_ref[...])

    projected = project_in(0)
    for sub in range(LRU_SUBTILES):
        t0 = sub * ts
        h, gr = projected
        if sub + 1 < LRU_SUBTILES:
            projected = project_in(sub + 1)
        gate = _gelu_tanh(gr[:, :LRU_WIDTH])
        xr = gr[:, LRU_WIDTH:]
        ext = jnp.concatenate([x_prev, xr], axis=0)
        x_prev = xr[rows - halo:]
        rec = cb_ref[...] + cw[LRU_CONV_WIDTH - 1:LRU_CONV_WIDTH] * xr
        for k in range(LRU_CONV_WIDTH - 1):
            rec = rec + cw[k:k + 1] * ext[k * nb:k * nb + rows]

        outs = []
        for n in range(LRU_BLOCKS):
            sl = slice(n * LRU_BLOCK_W, (n + 1) * LRU_BLOCK_W)
            xb = rec[:, sl]
            ri = _dot(xb.astype(BF16), wax_ref[n])
            neg_log_a = neg_half_c_lsl[:, sl] * (jnp.tanh(ri[:, :LRU_BLOCK_W] + ba[:, sl]) + 1.0)
            i = 0.5 * jnp.tanh(ri[:, LRU_BLOCK_W:] + bx[:, sl]) + 0.5
            a = jnp.exp2(neg_log_a * (-LOG2E))
            z = jnp.tanh(neg_log_a) * (a * a + 1.0)
            mult = jnp.where(z > 0.0, z * lax.rsqrt(z), 0.0)
            b = mult * (i * xb)
            hp = h_prev[:, sl]
            steps = []
            for t in range(ts):
                hp = a[t * nb:(t + 1) * nb] * hp + b[t * nb:(t + 1) * nb]
                steps.append(hp)
            outs.append(jnp.concatenate(steps, axis=0))
        hs = jnp.concatenate(outs, axis=1)
        h_prev = hs[rows - nb:]
        y = _dot((gate * hs).astype(BF16), wout_ref[...])
        o_ref[t0:t0 + ts] = (h + y).reshape(ts, nb, d)

    xcarry_ref[...] = x_prev
    hcarry_ref[...] = h_prev


def _lru_layer(h, g, w_in, cw, cb, wax, ba, bx, lam, w_out):
    seq, bsz, d = h.shape
    assert bsz == SUBLANES
    tt = TT_LRU
    consts = (g, w_in, cw, cb, wax, ba, bx, lam, w_out)
    tok = pl.BlockSpec((tt, bsz, d), lambda s: (s, 0, 0))
    return pl.pallas_call(
        _lru_layer_kernel,
        out_shape=jax.ShapeDtypeStruct(h.shape, h.dtype),
        grid=(seq // tt,),
        in_specs=[tok] + [_const_spec(c.shape) for c in consts],
        out_specs=tok,
        scratch_shapes=[pltpu.VMEM(((LRU_CONV_WIDTH - 1) * bsz, LRU_WIDTH), F32),
                        pltpu.VMEM((bsz, LRU_WIDTH), F32)],
        compiler_params=pltpu.CompilerParams(
            dimension_semantics=("arbitrary",), vmem_limit_bytes=VMEM_LIMIT),
        name="lru_layer",
    )(h, *consts)


def _mla_proj_kernel(h_ref, pos_ref, g_ref, wd_ref, qg_ref, kvg_ref, wuq_ref, wukv_ref,
                     qng_ref, qrg_ref, kng_ref, krg_ref, freq_ref, sign_ref, live_ref, pair_ref,
                     q_ref, k_ref, v_ref):
    ts = h_ref.shape[1] // MLA_SUBTILES
    nope_all = MLA_HEADS * QK_NOPE_DIM
    half = QK_ROPE_DIM // 2
    q_scale = LOG2E / math.sqrt(QK_NOPE_DIM + QK_ROPE_DIM)
    qn_gain = qng_ref[...] * q_scale
    qr_gain = qrg_ref[...] * q_scale
    ones = jnp.ones((ts, LANES), BF16)

    def slab_norm_scales(x):
        scales = []
        for p in range(x.shape[1] // (2 * LANES)):
            xs = x[:, p * 2 * LANES:(p + 1) * 2 * LANES]
            r = lax.rsqrt(_dot((xs * xs).astype(BF16), pair_ref[...]) + NORM_EPS)
            scales += [r[:, :LANES], r[:, LANES:]]
        return scales

    def project(sub):
        rows = pl.ds(sub * ts, ts)
        xn = _rms(h_ref[0, rows, :], g_ref[...]).astype(BF16)
        c = _dot(xn, wd_ref[...])
        c_q = c[:, :Q_LORA_RANK]
        c_kv = c[:, Q_LORA_RANK:Q_LORA_RANK + KV_LORA_RANK]
        k_rope = c[:, Q_LORA_RANK + KV_LORA_RANK:]
        q = _dot(_rms(c_q, qg_ref[...]).astype(BF16), wuq_ref[...])
        kv = _dot(_rms(c_kv, kvg_ref[...]).astype(BF16), wukv_ref[...])
        rq_nope = slab_norm_scales(q[:, :nope_all])
        rq_rope = slab_norm_scales(q[:, nope_all:])
        rk_nope = slab_norm_scales(kv[:, :nope_all])
        rk_rope = lax.rsqrt(_dot((k_rope * k_rope).astype(BF16), pair_ref[:LANES, :LANES]) + NORM_EPS)
        return q, kv, k_rope, rq_nope, rq_rope, rk_nope, rk_rope

    def finish(sub, projected):
        q, kv, k_rope, rq_nope, rq_rope, rk_nope, rk_rope = projected
        rows = pl.ds(sub * ts, ts)
        ang = pos_ref[0, rows, :] * freq_ref[...]
        cos = jnp.cos(ang) * live_ref[...]
        sin = jnp.sin(ang) * sign_ref[...]

        def rope(y):
            return y * cos + pltpu.roll(y, half, 1) * sin

        k_rope = rope(k_rope * rk_rope * krg_ref[...]).astype(BF16)
        for hh in range(MLA_HEADS):
            sl = slice(hh * LANES, (hh + 1) * LANES)
            sl2 = slice(nope_all + hh * LANES, nope_all + (hh + 1) * LANES)
            q_ref[0, hh, rows, :LANES] = (q[:, sl] * rq_nope[hh] * qn_gain).astype(BF16)
            q_ref[0, hh, rows, LANES:] = rope(q[:, sl2] * rq_rope[hh] * qr_gain).astype(BF16)
            k_ref[0, hh, rows, :LANES] = (kv[:, sl] * rk_nope[hh] * kng_ref[...]).astype(BF16)
            k_ref[0, hh, rows, LANES:] = k_rope
            v_ref[0, hh, rows, :LANES] = kv[:, sl2].astype(BF16)
            v_ref[0, hh, rows, LANES:] = ones

    projected = project(0)
    for sub in range(MLA_SUBTILES):
        current = projected
        if sub + 1 < MLA_SUBTILES:
            projected = project(sub + 1)
        finish(sub, current)


def _mla_proj(h, posb, consts):
    bsz, seq, d = h.shape
    ts = TS_MLA
    out_sds = jax.ShapeDtypeStruct((bsz, MLA_HEADS, seq, QK_DIM_PADDED), BF16)
    out_spec = pl.BlockSpec((1, MLA_HEADS, ts, QK_DIM_PADDED), lambda b, s: (b, 0, s, 0))
    return pl.pallas_call(
        _mla_proj_kernel,
        out_shape=(out_sds, out_sds, out_sds),
        grid=(bsz, seq // ts),
        in_specs=[pl.BlockSpec((1, ts, d), lambda b, s: (b, s, 0)),
                  pl.BlockSpec((1, ts, LANES), lambda b, s: (b, s, 0))]
        + [_const_spec(c.shape) for c in consts],
        out_specs=(out_spec, out_spec, out_spec),
        compiler_params=pltpu.CompilerParams(
            dimension_semantics=("arbitrary", "arbitrary"), vmem_limit_bytes=VMEM_LIMIT),
        name="mla_proj",
    )(h, posb, *consts)


def _attn_kernel(q_ref, k_ref, v_ref, o_ref, m_ref, acc_ref, s0_ref, s1_ref):
    qi = pl.program_id(2)
    nhp, tq = q_ref.shape[1], q_ref.shape[2]
    tk = tq
    ng = nhp // 2
    m_ref[...] = jnp.full_like(m_ref, -jnp.inf)
    acc_ref[...] = jnp.zeros_like(acc_ref)

    def logits(kb, group, s_ref):
        start = pl.multiple_of(kb * tk, tk)
        for i in range(ng):
            hp = group * ng + i
            k = k_ref[0, hp, pl.ds(start, tk), :]
            s_ref[i] = lax.dot_general(q_ref[0, hp], k, (((1,), (1,)), ((), ())),
                                       preferred_element_type=F32)

    def softmax_pv(kb, group, s_ref, masked):
        start = pl.multiple_of(kb * tk, tk)
        for i in range(ng):
            hp = group * ng + i
            s = s_ref[i]
            if masked:
                row = lax.broadcasted_iota(jnp.int32, s.shape, 0)
                col = lax.broadcasted_iota(jnp.int32, s.shape, 1)
                s = jnp.where(col <= row, s, jnp.finfo(F32).min)
            m_prev = m_ref[hp]
            m_next = jnp.maximum(m_prev, jnp.max(s, axis=1, keepdims=True))
            alpha = jnp.exp2(m_prev - m_next)
            p = jnp.exp2(s - jnp.concatenate([m_next] * (tk // LANES), axis=1))
            v = v_ref[0, hp, pl.ds(start, tk), :]
            acc_ref[hp] = (jnp.concatenate([alpha, alpha], axis=1) * acc_ref[hp]
                           + _dot(p.astype(BF16), v))
            m_ref[hp] = m_next

    logits(0, 0, s0_ref)

    def unmasked_block(kb):
        logits(kb, 1, s1_ref)
        softmax_pv(kb, 0, s0_ref, False)
        logits(kb + 1, 0, s0_ref)
        softmax_pv(kb, 1, s1_ref, False)

    odd = qi % 2

    @pl.when(odd == 1)
    def _():
        unmasked_block(0)

    def body(j, carry):
        unmasked_block(odd + 2 * j)
        unmasked_block(odd + 2 * j + 1)
        return carry

    lax.fori_loop(0, qi // 2, body, 0)
    logits(qi, 1, s1_ref)
    softmax_pv(qi, 0, s0_ref, True)
    softmax_pv(qi, 1, s1_ref, True)

    for hp in range(nhp):
        acc = acc_ref[hp]
        o_ref[0, :, hp * LANES:(hp + 1) * LANES] = (acc[:, :LANES] / acc[:, LANES:]).astype(o_ref.dtype)


def _attention(q, k, v1):
    bsz, nh, seq, dq = q.shape
    tq = TQ_ATTN
    hp = ATTN_HEADS_PER_STEP
    return pl.pallas_call(
        _attn_kernel,
        out_shape=jax.ShapeDtypeStruct((bsz, seq, nh * V_HEAD_DIM), BF16),
        grid=(bsz, nh // hp, seq // tq),
        in_specs=[pl.BlockSpec((1, hp, tq, dq), lambda b, h, i: (b, h, i, 0)),
                  pl.BlockSpec((1, hp, seq, dq), lambda b, h, i: (b, h, 0, 0)),
                  pl.BlockSpec((1, hp, seq, dq), lambda b, h, i: (b, h, 0, 0))],
        out_specs=pl.BlockSpec((1, tq, hp * V_HEAD_DIM), lambda b, h, i: (b, i, h)),
        scratch_shapes=[pltpu.VMEM((hp, tq, LANES), F32), pltpu.VMEM((hp, tq, dq), F32),
                        pltpu.VMEM((hp // 2, tq, tq), F32), pltpu.VMEM((hp // 2, tq, tq), F32)],
        compiler_params=pltpu.CompilerParams(
            dimension_semantics=("arbitrary", "arbitrary", "arbitrary"),
            vmem_limit_bytes=VMEM_LIMIT),
        name="mla_attention",
    )(q, k, v1)


def _ffn_subtiles(load, store, g_ref, wgu_ref, wdn_ref):
    def prepare(i):
        h = load(i)
        return h, _rms(h, g_ref[...]).astype(BF16)

    prepared = prepare(0)
    for i in range(FFN_SUBTILES):
        h, xn = prepared
        out = h
        for ci, (lo, hi) in enumerate(FFN_CHUNK_BOUNDS):
            gate = _dot(xn, wgu_ref[:, lo:hi])
            up = _dot(xn, wgu_ref[:, D_FF + lo:D_FF + hi])
            if ci == 0 and i + 1 < FFN_SUBTILES:
                prepared = prepare(i + 1)
            act = (jax.nn.silu(gate) * up).astype(BF16)
            out = out + _dot(act, wdn_ref[lo:hi, :])
        store(i, out)


def _ffn_kernel(h_ref, g_ref, wgu_ref, wdn_ref, *rest):
    cast_srcs, o_ref, cast_dsts = _split_cast_refs(rest)
    _run_casts(cast_srcs, cast_dsts)
    rows = h_ref.shape[0] // FFN_SUBTILES

    def store(i, out):
        o_ref[i * rows:(i + 1) * rows] = out

    _ffn_subtiles(lambda i: h_ref[i * rows:(i + 1) * rows], store, g_ref, wgu_ref, wdn_ref)


def _ffn_from_time_major_kernel(h_ref, g_ref, wgu_ref, wdn_ref, *rest):
    cast_srcs, o_ref, cast_dsts = _split_cast_refs(rest)
    _run_casts(cast_srcs, cast_dsts)
    tt, nb, d = h_ref.shape
    ts = tt // FFN_SUBTILES

    def store(i, out):
        o_ref[:, i * ts:(i + 1) * ts, :] = jnp.swapaxes(out.reshape(ts, nb, d), 0, 1)

    _ffn_subtiles(lambda i: h_ref[i * ts:(i + 1) * ts].reshape(ts * nb, d), store,
                  g_ref, wgu_ref, wdn_ref)


def _proj_ffn_kernel(h_ref, a_ref, wo_ref, g_ref, wgu_ref, wdn_ref, *rest):
    cast_srcs, o_ref, cast_dsts = _split_cast_refs(rest)
    _run_casts(cast_srcs, cast_dsts)
    rows = h_ref.shape[0] // FFN_SUBTILES

    def load(i):
        sl = slice(i * rows, (i + 1) * rows)
        return h_ref[sl] + _dot(a_ref[sl], wo_ref[...])

    def store(i, out):
        o_ref[i * rows:(i + 1) * rows] = out

    _ffn_subtiles(load, store, g_ref, wgu_ref, wdn_ref)


def _ffn_call(kernel, name, n_steps, tokens, token_specs, consts, out_sds, out_spec, cast):
    cast_in, cast_out, cast_shapes = _cast_side_job(cast, n_steps)
    out = pl.pallas_call(
        kernel,
        out_shape=[out_sds] + cast_shapes,
        grid=(n_steps,),
        in_specs=list(token_specs) + [_const_spec(c.shape) for c in consts] + cast_in,
        out_specs=[out_spec] + cast_out,
        compiler_params=pltpu.CompilerParams(
            dimension_semantics=("arbitrary",), vmem_limit_bytes=VMEM_LIMIT),
        name=name,
    )(*tokens, *consts, *[stacked for stacked, _ in cast])
    return out[0], tuple(out[1:])


def _ffn_from_time_major(h, g, w_gu, w_dn, cast=()):
    seq, bsz, d = h.shape
    tt = TM_FFN // bsz
    return _ffn_call(_ffn_from_time_major_kernel, "ffn_tm", seq // tt, (h,),
                     [pl.BlockSpec((tt, bsz, d), lambda s: (s, 0, 0))], (g, w_gu, w_dn),
                     jax.ShapeDtypeStruct((bsz, seq, d), h.dtype),
                     pl.BlockSpec((bsz, tt, d), lambda s: (0, s, 0)), cast)


def _ffn(h2, g, w_gu, w_dn, cast=()):
    t, d = h2.shape
    tok = pl.BlockSpec((TM_FFN, d), lambda i: (i, 0))
    return _ffn_call(_ffn_kernel, "ffn", t // TM_FFN, (h2,), [tok], (g, w_gu, w_dn),
                     jax.ShapeDtypeStruct(h2.shape, h2.dtype), tok, cast)


def _proj_ffn(h2, attn, w_o, g, w_gu, w_dn, cast=()):
    t, d = h2.shape
    tok = pl.BlockSpec((TM_FFN, d), lambda i: (i, 0))
    return _ffn_call(_proj_ffn_kernel, "proj_ffn", t // TM_FFN, (h2, attn),
                     [tok, pl.BlockSpec((TM_FFN, attn.shape[1]), lambda i: (i, 0))],
                     (w_o, g, w_gu, w_dn), jax.ShapeDtypeStruct((t, d), h2.dtype), tok, cast)


def _row(v):
    return v.reshape(1, -1)


def _pad_lanes(v, n):
    return jnp.pad(v, [(0, 0)] * (v.ndim - 1) + [(0, n - v.shape[-1])])


def _dup(v):
    return jnp.concatenate([v, v], axis=-1)


def kernel(x, positions, mix_norm, conv_w_in, conv_w, conv_w_out, lru_w_in, lru_conv_w, lru_conv_b, lru_gate_a_w, lru_gate_a_b, lru_gate_x_w, lru_gate_x_b, lru_lambda, lru_w_out, mla_w_down, mla_q_norm, mla_kv_norm, mla_w_uq, mla_w_ukv, mla_qn_norm, mla_qr_norm, mla_kn_norm, mla_kr_norm, mla_w_o, ffn_norm, ffn_w_gu, ffn_w_down):
    bsz, seq, d = x.shape
    nh = MLA_HEADS
    half = QK_ROPE_DIM // 2

    inv_freq = ROPE_THETA ** (-jnp.arange(0, QK_ROPE_DIM, 2, dtype=F32) / QK_ROPE_DIM)
    freq_slab = _row(_pad_lanes(_dup(inv_freq), LANES))
    sign_slab = _row(_pad_lanes(jnp.concatenate([-jnp.ones(half, F32), jnp.ones(half, F32)]), LANES))
    live_slab = _row(_pad_lanes(jnp.ones(QK_ROPE_DIM, F32), LANES))
    lane_slab = jnp.arange(2 * LANES) // LANES
    pair_ones = ((lane_slab[:, None] == lane_slab[None, :]) * (1.0 / LANES)).astype(BF16)
    posb = jnp.broadcast_to(positions.astype(F32)[..., None], (bsz, seq, LANES))

    def mixer_jobs(i):
        kind, j = i % N_MIXERS, i // N_MIXERS
        if kind == 0:
            return ((conv_w_in, j), (conv_w_out, j))
        if kind == 1:
            return ((lru_w_in, j), (lru_w_out, j))
        return ((mla_w_o, j),)

    def ffn_jobs(i):
        return ((ffn_w_gu, i), (ffn_w_down, i))

    def cast_now(jobs):
        return tuple(stacked[index].astype(BF16) for stacked, index in jobs)

    h = x
    mixer_w = cast_now(mixer_jobs(0))
    ffn_w = None
    for i in range(DEPTH):
        kind, j = i % N_MIXERS, i // N_MIXERS
        g = _row(mix_norm[i])
        gf = _row(ffn_norm[i])
        last = i + 1 == DEPTH
        next_jobs = () if last else mixer_jobs(i + 1) + ffn_jobs(i + 1)
        to_time_major = kind == 0 and not last and (i + 1) % N_MIXERS == 1
        if ffn_w is None and not to_time_major:
            ffn_w = cast_now(ffn_jobs(i))
        if kind == 0:
            conv_args = (h, g, mixer_w[0], conv_w[j], mixer_w[1])
            if to_time_major:
                h, cast_ffn = _conv_layer_to_time_major(
                    *conv_args, cast=() if ffn_w else ffn_jobs(i))
                h, casts = _ffn(h.reshape(seq * bsz, d), gf, *(ffn_w or cast_ffn), cast=next_jobs)
                h = h.reshape(seq, bsz, d)
            else:
                h = _conv_layer(*conv_args)
                h, casts = _ffn(h.reshape(bsz * seq, d), gf, *ffn_w, cast=next_jobs)
                h = h.reshape(bsz, seq, d)
        elif kind == 1:
            wax = (0.5 * jnp.concatenate([lru_gate_a_w[j], lru_gate_x_w[j]], axis=-1)).astype(BF16)
            h = _lru_layer(h, g, mixer_w[0], lru_conv_w[j], _row(lru_conv_b[j]),
                           wax, _row(0.5 * lru_gate_a_b[j]), _row(0.5 * lru_gate_x_b[j]),
                           _row(lru_lambda[j]), mixer_w[1])
            h, casts = _ffn_from_time_major(h, gf, *ffn_w, cast=next_jobs)
        else:
            wdm = mla_w_down[j]
            wd = jnp.concatenate([wdm, wdm[:, Q_LORA_RANK + KV_LORA_RANK:]], axis=1).astype(BF16)
            wq = mla_w_uq[j].reshape(Q_LORA_RANK, nh, QK_NOPE_DIM + QK_ROPE_DIM)
            wuq = jnp.concatenate(
                [wq[:, :, :QK_NOPE_DIM].reshape(Q_LORA_RANK, nh * QK_NOPE_DIM),
                 _dup(wq[:, :, QK_NOPE_DIM:]).reshape(Q_LORA_RANK, nh * LANES)],
                axis=1).astype(BF16)
            wkv = mla_w_ukv[j].reshape(KV_LORA_RANK, nh, QK_NOPE_DIM + V_HEAD_DIM)
            wukv = jnp.concatenate(
                [wkv[:, :, :QK_NOPE_DIM].reshape(KV_LORA_RANK, nh * QK_NOPE_DIM),
                 wkv[:, :, QK_NOPE_DIM:].reshape(KV_LORA_RANK, nh * V_HEAD_DIM)], axis=1).astype(BF16)
            consts = (g, wd, _row(mla_q_norm[j]), _row(mla_kv_norm[j]), wuq, wukv,
                      _row(mla_qn_norm[j]), _row(_dup(mla_qr_norm[j])),
                      _row(mla_kn_norm[j]), _row(_dup(mla_kr_norm[j])),
                      freq_slab, sign_slab, live_slab, pair_ones)
            q, k, v1 = _mla_proj(h, posb, consts)
            attn = _attention(q, k, v1).reshape(bsz * seq, nh * V_HEAD_DIM)
            h, casts = _proj_ffn(h.reshape(bsz * seq, d), attn, mixer_w[0], gf, *ffn_w,
                                 cast=next_jobs)
            h = h.reshape(bsz, seq, d)
        if not last:
            n_mixer = len(mixer_jobs(i + 1))
            mixer_w, ffn_w = casts[:n_mixer], casts[n_mixer:]
    return h
```

```python
import math

import jax
import jax.numpy as jnp
from jax import lax
from jax.experimental import pallas as pl
from jax.experimental.pallas import tpu as pltpu

F32 = jnp.float32
BF16 = jnp.bfloat16

D_MODEL = 1024
DEPTH = 4
N_MIXERS = 3
CONV_WIDTH = 3
LRU_WIDTH = 1280
LRU_BLOCKS = 10
LRU_BLOCK_W = LRU_WIDTH // LRU_BLOCKS
LRU_CONV_WIDTH = 4
LRU_C = 8.0
MLA_HEADS = 8
Q_LORA_RANK = 384
KV_LORA_RANK = 256
QK_NOPE_DIM = 128
QK_ROPE_DIM = 64
V_HEAD_DIM = 128
ROPE_THETA = 10000.0
D_FF = 2816
NORM_EPS = 1e-6

LANES = 128
SUBLANES = 8
BF16_SUBLANES = 16
QK_DIM_PADDED = 2 * LANES
VMEM_LIMIT = 56 * 1024 * 1024

TS_CONV = 1024
CONV_SUBTILES = 2
TT_LRU = 128
LRU_SUBTILES = 2
TS_MLA = 1024
ROPE_TABLE_POS_ROWS = 32
MLA_SUBTILES = 4
TQ_ATTN = 512
ATTN_HEADS_PER_STEP = 4
TM_FFN = 1024
FFN_SUBTILES = 2
MXU_DIM = 256
_FFN_SPLIT = (D_FF // MXU_DIM + 1) // 2 * MXU_DIM
FFN_CHUNK_BOUNDS = ((0, _FFN_SPLIT), (_FFN_SPLIT, D_FF))
LOG2E = 1.4426950408889634


def _rms(x, g):
    return x * lax.rsqrt(jnp.mean(x * x, axis=-1, keepdims=True) + NORM_EPS) * g


def _dot(a, b):
    return jnp.dot(a, b, preferred_element_type=F32)


def _gelu_tanh(x):
    k0 = -2.0 * LOG2E * math.sqrt(2.0 / math.pi)
    return x / (1.0 + jnp.exp2(x * (k0 + (k0 * 0.044715) * (x * x))))


def _cast_side_job(jobs, n_steps):
    in_specs, out_specs, out_shapes = [], [], []
    for stacked, index in jobs:
        _, rows, cols = stacked.shape
        n_blocks = n_steps
        while rows % n_blocks or (rows // n_blocks) % BF16_SUBLANES:
            n_blocks //= 2
        rep = n_steps // n_blocks
        in_specs.append(pl.BlockSpec((None, rows // n_blocks, cols),
                                     lambda i, rep=rep, index=index: (index, i // rep, 0)))
        out_specs.append(pl.BlockSpec((rows // n_blocks, cols), lambda i, rep=rep: (i // rep, 0)))
        out_shapes.append(jax.ShapeDtypeStruct((rows, cols), BF16))
    return in_specs, out_specs, out_shapes


def _split_cast_refs(refs):
    n = (len(refs) - 1) // 2
    return refs[:n], refs[n], refs[n + 1:]


def _run_casts(srcs, dsts):
    for src, dst in zip(srcs, dsts):
        dst[...] = src[...].astype(dst.dtype)


def _const_spec(shape):
    nd = len(shape)
    return pl.BlockSpec(shape, lambda *_: (0,) * nd, pipeline_mode=pl.Buffered(1))


def _shift_rows_prev(x, prev8, d):
    r = pltpu.roll(x, d, 0)
    hd = pltpu.roll(prev8, d, 0)
    rows8 = lax.broadcasted_iota(jnp.int32, prev8.shape, 0)
    first = jnp.where(rows8 < d, hd, r[:SUBLANES])
    return jnp.concatenate([first, r[SUBLANES:]], axis=0)


def _conv_layer_kernel(h_ref, g_ref, win_ref, cw_ref, wout_ref, o_ref, carry_ref):
    @pl.when(pl.program_id(1) == 0)
    def _():
        carry_ref[...] = jnp.zeros_like(carry_ref)

    rows = h_ref.shape[1] // CONV_SUBTILES
    cw = cw_ref[...]

    def project_in(sub):
        h = h_ref[0, sub * rows:(sub + 1) * rows, :]
        return h, _dot(_rms(h, g_ref[...]).astype(BF16), win_ref[...])

    prev8 = carry_ref[...]
    projected = project_in(0)
    for sub in range(CONV_SUBTILES):
        h, bch = projected
        if sub + 1 < CONV_SUBTILES:
            projected = project_in(sub + 1)
        b_gate = bch[:, :D_MODEL]
        u = bch[:, D_MODEL:2 * D_MODEL] * bch[:, 2 * D_MODEL:]
        conv = (cw[0:1] * _shift_rows_prev(u, prev8, 2)
                + cw[1:2] * _shift_rows_prev(u, prev8, 1)
                + cw[2:3] * u)
        prev8 = u[rows - SUBLANES:]
        y = _dot((b_gate * conv).astype(BF16), wout_ref[...])
        o_ref[0, sub * rows:(sub + 1) * rows, :] = h + y
    carry_ref[...] = prev8


def _conv_layer(h, g, w_in, cw, w_out):
    bsz, seq, d = h.shape
    ts = TS_CONV
    tok = pl.BlockSpec((1, ts, d), lambda b, s: (b, s, 0))
    return pl.pallas_call(
        _conv_layer_kernel,
        out_shape=jax.ShapeDtypeStruct(h.shape, h.dtype),
        grid=(bsz, seq // ts),
        in_specs=[
            tok, _const_spec(g.shape), _const_spec(w_in.shape), _const_spec(cw.shape),
            _const_spec(w_out.shape),
        ],
        out_specs=tok,
        scratch_shapes=[pltpu.VMEM((SUBLANES, d), F32)],
        compiler_params=pltpu.CompilerParams(
            dimension_semantics=("arbitrary", "arbitrary"), vmem_limit_bytes=VMEM_LIMIT),
        name="conv_layer",
    )(h, g, w_in, cw, w_out)


def _conv_layer_tm_kernel(h_ref, g_ref, win_ref, cw_ref, wout_ref, *rest):
    cast_srcs, o_ref, cast_dsts = _split_cast_refs(rest[:-1])
    carry_ref = rest[-1]
    _run_casts(cast_srcs, cast_dsts)

    @pl.when(pl.program_id(0) == 0)
    def _():
        carry_ref[...] = jnp.zeros_like(carry_ref)

    nb, tt, d = h_ref.shape
    ts = tt // CONV_SUBTILES
    rows = ts * nb
    halo = (CONV_WIDTH - 1) * nb
    cw = cw_ref[...]

    def project_in(sub):
        h = jnp.swapaxes(h_ref[:, sub * ts:(sub + 1) * ts, :], 0, 1).reshape(rows, d)
        return h, _dot(_rms(h, g_ref[...]).astype(BF16), win_ref[...])

    u_prev = carry_ref[...]
    projected = project_in(0)
    for sub in range(CONV_SUBTILES):
        h, bch = projected
        if sub + 1 < CONV_SUBTILES:
            projected = project_in(sub + 1)
        b_gate = bch[:, :D_MODEL]
        u = bch[:, D_MODEL:2 * D_MODEL] * bch[:, 2 * D_MODEL:]
        ext = jnp.concatenate([u_prev, u], axis=0)
        u_prev = u[rows - halo:]
        conv = cw[CONV_WIDTH - 1:CONV_WIDTH] * u
        for k in range(CONV_WIDTH - 1):
            conv = conv + cw[k:k + 1] * ext[k * nb:k * nb + rows]
        y = _dot((b_gate * conv).astype(BF16), wout_ref[...])
        o_ref[sub * ts:(sub + 1) * ts] = (h + y).reshape(ts, nb, d)
    carry_ref[...] = u_prev


def _conv_layer_to_time_major(h, g, w_in, cw, w_out, cast=()):
    bsz, seq, d = h.shape
    assert bsz == SUBLANES
    tt = TS_CONV // bsz
    cast_in, cast_out, cast_shapes = _cast_side_job(cast, seq // tt)
    out = pl.pallas_call(
        _conv_layer_tm_kernel,
        out_shape=[jax.ShapeDtypeStruct((seq, bsz, d), h.dtype)] + cast_shapes,
        grid=(seq // tt,),
        in_specs=[
            pl.BlockSpec((bsz, tt, d), lambda s: (0, s, 0)),
            _const_spec(g.shape), _const_spec(w_in.shape), _const_spec(cw.shape),
            _const_spec(w_out.shape),
        ] + cast_in,
        out_specs=[pl.BlockSpec((tt, bsz, d), lambda s: (s, 0, 0))] + cast_out,
        scratch_shapes=[pltpu.VMEM(((CONV_WIDTH - 1) * bsz, d), F32)],
        compiler_params=pltpu.CompilerParams(
            dimension_semantics=("arbitrary",), vmem_limit_bytes=VMEM_LIMIT),
        name="conv_layer_tm",
    )(h, g, w_in, cw, w_out, *[stacked for stacked, _ in cast])
    return out[0], tuple(out[1:])


def _lru_layer_kernel(h_ref, g_ref, win_ref, cw_ref, cb_ref, wax_ref, ba_ref, bx_ref, lam_ref,
                      wout_ref, o_ref, xcarry_ref, hcarry_ref):
    @pl.when(pl.program_id(0) == 0)
    def _():
        xcarry_ref[...] = jnp.zeros_like(xcarry_ref)
        hcarry_ref[...] = jnp.zeros_like(hcarry_ref)

    tt, nb, d = h_ref.shape
    ts = tt // LRU_SUBTILES
    rows = ts * nb
    halo = (LRU_CONV_WIDTH - 1) * nb
    cw = cw_ref[...]
    lam = lam_ref[...]
    log_sig_lam = jnp.minimum(lam, 0.0) - jnp.log1p(jnp.exp(-jnp.abs(lam)))
    neg_half_c_lsl = (-0.5 * LRU_C) * log_sig_lam
    ba = ba_ref[...]
    bx = bx_ref[...]
    x_prev = xcarry_ref[...]
    h_prev = hcarry_ref[...]

    def project_in(sub):
        h = h_ref[sub * ts:(sub + 1) * ts].reshape(rows, d)
        xn = _rms(h, g_ref[...]).astype(BF16)
        return h, _dot(xn, win_ref[...])

    projected = project_in(0)
    for sub in range(LRU_SUBTILES):
        t0 = sub * ts
        h, gr = projected
        if sub + 1 < LRU_SUBTILES:
            projected = project_in(sub + 1)
        gate = _gelu_tanh(gr[:, :LRU_WIDTH])
        xr = gr[:, LRU_WIDTH:]
        ext = jnp.concatenate([x_prev, xr], axis=0)
        x_prev = xr[rows - halo:]
        rec = cb_ref[...] + cw[LRU_CONV_WIDTH - 1:LRU_CONV_WIDTH] * xr
        for k in range(LRU_CONV_WIDTH - 1):
            rec = rec + cw[k:k + 1] * ext[k * nb:k * nb + rows]

        outs = []
        for n in range(LRU_BLOCKS):
            sl = slice(n * LRU_BLOCK_W, (n + 1) * LRU_BLOCK_W)
            xb = rec[:, sl]
            ri = _dot(xb.astype(BF16), wax_ref[n])
            neg_log_a = neg_half_c_lsl[:, sl] * (jnp.tanh(ri[:, :LRU_BLOCK_W] + ba[:, sl]) + 1.0)
            i = 0.5 * jnp.tanh(ri[:, LRU_BLOCK_W:] + bx[:, sl]) + 0.5
            a = jnp.exp2(neg_log_a * (-LOG2E))
            z = jnp.tanh(neg_log_a) * (a * a + 1.0)
            mult = jnp.where(z > 0.0, z * lax.rsqrt(z), 0.0)
            b = mult * (i * xb)
            hp = h_prev[:, sl]
            steps = []
            for t in range(ts):
                hp = a[t * nb:(t + 1) * nb] * hp + b[t * nb:(t + 1) * nb]
                steps.append(hp)
            outs.append(jnp.concatenate(steps, axis=0))
        hs = jnp.concatenate(outs, axis=1)
        h_prev = hs[rows - nb:]
        y = _dot((gate * hs).astype(BF16), wout_ref[...])
        o_ref[t0:t0 + ts] = (h + y).reshape(ts, nb, d)

    xcarry_ref[...] = x_prev
    hcarry_ref[...] = h_prev


def _lru_layer(h, g, w_in, cw, cb, wax, ba, bx, lam, w_out):
    seq, bsz, d = h.shape
    assert bsz == SUBLANES
    tt = TT_LRU
    consts = (g, w_in, cw, cb, wax, ba, bx, lam, w_out)
    tok = pl.BlockSpec((tt, bsz, d), lambda s: (s, 0, 0))
    return pl.pallas_call(
        _lru_layer_kernel,
        out_shape=jax.ShapeDtypeStruct(h.shape, h.dtype),
        grid=(seq // tt,),
        in_specs=[tok] + [_const_spec(c.shape) for c in consts],
        out_specs=tok,
        scratch_shapes=[pltpu.VMEM(((LRU_CONV_WIDTH - 1) * bsz, LRU_WIDTH), F32),
                        pltpu.VMEM((bsz, LRU_WIDTH), F32)],
        compiler_params=pltpu.CompilerParams(
            dimension_semantics=("arbitrary",), vmem_limit_bytes=VMEM_LIMIT),
        name="lru_layer",
    )(h, *consts)


def _rope_table_kernel(pos_ref, freq_ref, live_ref, sign_ref, cos_ref, sin_ref):
    for r in range(pos_ref.shape[0]):
        ang = freq_ref[...] * pos_ref[r:r + 1, :]
        reps = LANES // ang.shape[0]
        cos_t = jnp.concatenate([jnp.cos(ang)] * reps, axis=0).T
        sin_t = jnp.concatenate([jnp.sin(ang)] * reps, axis=0).T
        cos_ref[r * LANES:(r + 1) * LANES, :] = cos_t * live_ref[...]
        sin_ref[r * LANES:(r + 1) * LANES, :] = sin_t * sign_ref[...]


def _rope_tables(positions, inv_freq, live_slab, sign_slab):
    bsz, seq = positions.shape
    n = bsz * seq
    pos2d = positions.astype(F32).reshape(n // LANES, LANES)
    rows = ROPE_TABLE_POS_ROWS * LANES
    table = jax.ShapeDtypeStruct((n, LANES), F32)
    small = lambda shape: pl.BlockSpec(shape, lambda i: (0, 0))
    cos, sin = pl.pallas_call(
        _rope_table_kernel,
        out_shape=[table, table],
        grid=(n // rows,),
        in_specs=[pl.BlockSpec((ROPE_TABLE_POS_ROWS, LANES), lambda i: (i, 0)),
                  small((inv_freq.shape[0], 1)), small((1, LANES)), small((1, LANES))],
        out_specs=[pl.BlockSpec((rows, LANES), lambda i: (i, 0))] * 2,
        name="rope_tables",
    )(pos2d, inv_freq.reshape(-1, 1), live_slab, sign_slab)
    return cos.reshape(bsz, seq, LANES), sin.reshape(bsz, seq, LANES)


def _mla_proj_kernel(h_ref, cos_ref, sin_ref, g_ref, wd_ref, qg_ref, kvg_ref, wuq_ref, wukv_ref,
                     qng_ref, qrg_ref, kng_ref, krg_ref, pair_ref,
                     q_ref, k_ref, v_ref):
    ts = h_ref.shape[1] // MLA_SUBTILES
    nope_all = MLA_HEADS * QK_NOPE_DIM
    half = QK_ROPE_DIM // 2
    q_scale = LOG2E / math.sqrt(QK_NOPE_DIM + QK_ROPE_DIM)
    qn_gain = qng_ref[...] * q_scale
    qr_gain = qrg_ref[...] * q_scale
    ones = jnp.ones((ts, LANES), BF16)

    def slab_norm_scales(x):
        scales = []
        for p in range(x.shape[1] // (2 * LANES)):
            xs = x[:, p * 2 * LANES:(p + 1) * 2 * LANES]
            r = lax.rsqrt(_dot((xs * xs).astype(BF16), pair_ref[...]) + NORM_EPS)
            scales += [r[:, :LANES], r[:, LANES:]]
        return scales

    def project(sub):
        rows = pl.ds(sub * ts, ts)
        xn = _rms(h_ref[0, rows, :], g_ref[...]).astype(BF16)
        c = _dot(xn, wd_ref[...])
        c_q = c[:, :Q_LORA_RANK]
        c_kv = c[:, Q_LORA_RANK:Q_LORA_RANK + KV_LORA_RANK]
        k_rope = c[:, Q_LORA_RANK + KV_LORA_RANK:]
        q = _dot(_rms(c_q, qg_ref[...]).astype(BF16), wuq_ref[...])
        kv = _dot(_rms(c_kv, kvg_ref[...]).astype(BF16), wukv_ref[...])
        rq_nope = slab_norm_scales(q[:, :nope_all])
        rq_rope = slab_norm_scales(q[:, nope_all:])
        rk_nope = slab_norm_scales(kv[:, :nope_all])
        rk_rope = lax.rsqrt(_dot((k_rope * k_rope).astype(BF16), pair_ref[:LANES, :LANES]) + NORM_EPS)
        return q, kv, k_rope, rq_nope, rq_rope, rk_nope, rk_rope

    def finish(sub, projected):
        q, kv, k_rope, rq_nope, rq_rope, rk_nope, rk_rope = projected
        rows = pl.ds(sub * ts, ts)
        cos = cos_ref[0, rows, :]
        sin = sin_ref[0, rows, :]

        def rope(y):
            return y * cos + pltpu.roll(y, half, 1) * sin

        k_rope = rope(k_rope * rk_rope * krg_ref[...]).astype(BF16)
        for hh in range(MLA_HEADS):
            sl = slice(hh * LANES, (hh + 1) * LANES)
            sl2 = slice(nope_all + hh * LANES, nope_all + (hh + 1) * LANES)
            q_ref[0, hh, rows, :LANES] = (q[:, sl] * rq_nope[hh] * qn_gain).astype(BF16)
            q_ref[0, hh, rows, LANES:] = rope(q[:, sl2] * rq_rope[hh] * qr_gain).astype(BF16)
            k_ref[0, hh, rows, :LANES] = (kv[:, sl] * rk_nope[hh] * kng_ref[...]).astype(BF16)
            k_ref[0, hh, rows, LANES:] = k_rope
            v_ref[0, hh, rows, :LANES] = kv[:, sl2].astype(BF16)
            v_ref[0, hh, rows, LANES:] = ones

    projected = project(0)
    for sub in range(MLA_SUBTILES):
        current = projected
        if sub + 1 < MLA_SUBTILES:
            projected = project(sub + 1)
        finish(sub, current)


def _mla_proj(h, cos_tab, sin_tab, consts):
    bsz, seq, d = h.shape
    ts = TS_MLA
    out_sds = jax.ShapeDtypeStruct((bsz, MLA_HEADS, seq, QK_DIM_PADDED), BF16)
    out_spec = pl.BlockSpec((1, MLA_HEADS, ts, QK_DIM_PADDED), lambda b, s: (b, 0, s, 0))
    slab_spec = pl.BlockSpec((1, ts, LANES), lambda b, s: (b, s, 0))
    return pl.pallas_call(
        _mla_proj_kernel,
        out_shape=(out_sds, out_sds, out_sds),
        grid=(bsz, seq // ts),
        in_specs=[pl.BlockSpec((1, ts, d), lambda b, s: (b, s, 0)), slab_spec, slab_spec]
        + [_const_spec(c.shape) for c in consts],
        out_specs=(out_spec, out_spec, out_spec),
        compiler_params=pltpu.CompilerParams(
            dimension_semantics=("arbitrary", "arbitrary"), vmem_limit_bytes=VMEM_LIMIT),
        name="mla_proj",
    )(h, cos_tab, sin_tab, *consts)


def _attn_kernel(q_ref, k_ref, v_ref, o_ref, m_ref, acc_ref, s0_ref, s1_ref):
    qi = pl.program_id(2)
    nhp, tq = q_ref.shape[1], q_ref.shape[2]
    tk = tq
    ng = nhp // 2
    m_ref[...] = jnp.full_like(m_ref, -jnp.inf)
    acc_ref[...] = jnp.zeros_like(acc_ref)

    def logits(kb, group, s_ref):
        start = pl.multiple_of(kb * tk, tk)
        for i in range(ng):
            hp = group * ng + i
            k = k_ref[0, hp, pl.ds(start, tk), :]
            s_ref[i] = lax.dot_general(q_ref[0, hp], k, (((1,), (1,)), ((), ())),
                                       preferred_element_type=F32)

    def softmax_pv(kb, group, s_ref, masked):
        start = pl.multiple_of(kb * tk, tk)
        for i in range(ng):
            hp = group * ng + i
            s = s_ref[i]
            if masked:
                row = lax.broadcasted_iota(jnp.int32, s.shape, 0)
                col = lax.broadcasted_iota(jnp.int32, s.shape, 1)
                s = jnp.where(col <= row, s, jnp.finfo(F32).min)
            m_prev = m_ref[hp]
            m_next = jnp.maximum(m_prev, jnp.max(s, axis=1, keepdims=True))
            alpha = jnp.exp2(m_prev - m_next)
            p = jnp.exp2(s - jnp.concatenate([m_next] * (tk // LANES), axis=1))
            v = v_ref[0, hp, pl.ds(start, tk), :]
            acc_ref[hp] = (jnp.concatenate([alpha, alpha], axis=1) * acc_ref[hp]
                           + _dot(p.astype(BF16), v))
            m_ref[hp] = m_next

    logits(0, 0, s0_ref)

    def unmasked_block(kb):
        logits(kb, 1, s1_ref)
        softmax_pv(kb, 0, s0_ref, False)
        logits(kb + 1, 0, s0_ref)
        softmax_pv(kb, 1, s1_ref, False)

    odd = qi % 2

    @pl.when(odd == 1)
    def _():
        unmasked_block(0)

    def body(j, carry):
        unmasked_block(odd + 2 * j)
        unmasked_block(odd + 2 * j + 1)
        return carry

    lax.fori_loop(0, qi // 2, body, 0)
    logits(qi, 1, s1_ref)
    softmax_pv(qi, 0, s0_ref, True)
    softmax_pv(qi, 1, s1_ref, True)

    for hp in range(nhp):
        acc = acc_ref[hp]
        o_ref[0, :, hp * LANES:(hp + 1) * LANES] = (acc[:, :LANES] / acc[:, LANES:]).astype(o_ref.dtype)


def _attention(q, k, v1):
    bsz, nh, seq, dq = q.shape
    tq = TQ_ATTN
    hp = ATTN_HEADS_PER_STEP
    return pl.pallas_call(
        _attn_kernel,
        out_shape=jax.ShapeDtypeStruct((bsz, seq, nh * V_HEAD_DIM), BF16),
        grid=(bsz, nh // hp, seq // tq),
        in_specs=[pl.BlockSpec((1, hp, tq, dq), lambda b, h, i: (b, h, i, 0)),
                  pl.BlockSpec((1, hp, seq, dq), lambda b, h, i: (b, h, 0, 0)),
                  pl.BlockSpec((1, hp, seq, dq), lambda b, h, i: (b, h, 0, 0))],
        out_specs=pl.BlockSpec((1, tq, hp * V_HEAD_DIM), lambda b, h, i: (b, i, h)),
        scratch_shapes=[pltpu.VMEM((hp, tq, LANES), F32), pltpu.VMEM((hp, tq, dq), F32),
                        pltpu.VMEM((hp // 2, tq, tq), F32), pltpu.VMEM((hp // 2, tq, tq), F32)],
        compiler_params=pltpu.CompilerParams(
            dimension_semantics=("arbitrary", "arbitrary", "arbitrary"),
            vmem_limit_bytes=VMEM_LIMIT),
        name="mla_attention",
    )(q, k, v1)


def _ffn_subtiles(load, store, g_ref, wgu_ref, wdn_ref):
    def prepare(i):
        h = load(i)
        return h, _rms(h, g_ref[...]).astype(BF16)

    prepared = prepare(0)
    for i in range(FFN_SUBTILES):
        h, xn = prepared
        out = h
        for ci, (lo, hi) in enumerate(FFN_CHUNK_BOUNDS):
            gate = _dot(xn, wgu_ref[:, lo:hi])
            up = _dot(xn, wgu_ref[:, D_FF + lo:D_FF + hi])
            if ci == 0 and i + 1 < FFN_SUBTILES:
                prepared = prepare(i + 1)
            act = (jax.nn.silu(gate) * up).astype(BF16)
            out = out + _dot(act, wdn_ref[lo:hi, :])
        store(i, out)


def _ffn_kernel(h_ref, g_ref, wgu_ref, wdn_ref, *rest):
    cast_srcs, o_ref, cast_dsts = _split_cast_refs(rest)
    _run_casts(cast_srcs, cast_dsts)
    rows = h_ref.shape[0] // FFN_SUBTILES

    def store(i, out):
        o_ref[i * rows:(i + 1) * rows] = out

    _ffn_subtiles(lambda i: h_ref[i * rows:(i + 1) * rows], store, g_ref, wgu_ref, wdn_ref)


def _ffn_from_time_major_kernel(h_ref, g_ref, wgu_ref, wdn_ref, *rest):
    cast_srcs, o_ref, cast_dsts = _split_cast_refs(rest)
    _run_casts(cast_srcs, cast_dsts)
    tt, nb, d = h_ref.shape
    ts = tt // FFN_SUBTILES

    def store(i, out):
        o_ref[:, i * ts:(i + 1) * ts, :] = jnp.swapaxes(out.reshape(ts, nb, d), 0, 1)

    _ffn_subtiles(lambda i: h_ref[i * ts:(i + 1) * ts].reshape(ts * nb, d), store,
                  g_ref, wgu_ref, wdn_ref)


def _proj_ffn_kernel(h_ref, a_ref, wo_ref, g_ref, wgu_ref, wdn_ref, *rest):
    cast_srcs, o_ref, cast_dsts = _split_cast_refs(rest)
    _run_casts(cast_srcs, cast_dsts)
    rows = h_ref.shape[0] // FFN_SUBTILES

    def load(i):
        sl = slice(i * rows, (i + 1) * rows)
        return h_ref[sl] + _dot(a_ref[sl], wo_ref[...])

    def store(i, out):
        o_ref[i * rows:(i + 1) * rows] = out

    _ffn_subtiles(load, store, g_ref, wgu_ref, wdn_ref)


def _ffn_call(kernel, name, n_steps, tokens, token_specs, consts, out_sds, out_spec, cast):
    cast_in, cast_out, cast_shapes = _cast_side_job(cast, n_steps)
    out = pl.pallas_call(
        kernel,
        out_shape=[out_sds] + cast_shapes,
        grid=(n_steps,),
        in_specs=list(token_specs) + [_const_spec(c.shape) for c in consts] + cast_in,
        out_specs=[out_spec] + cast_out,
        compiler_params=pltpu.CompilerParams(
            dimension_semantics=("arbitrary",), vmem_limit_bytes=VMEM_LIMIT),
        name=name,
    )(*tokens, *consts, *[stacked for stacked, _ in cast])
    return out[0], tuple(out[1:])


def _ffn_from_time_major(h, g, w_gu, w_dn, cast=()):
    seq, bsz, d = h.shape
    tt = TM_FFN // bsz
    return _ffn_call(_ffn_from_time_major_kernel, "ffn_tm", seq // tt, (h,),
                     [pl.BlockSpec((tt, bsz, d), lambda s: (s, 0, 0))], (g, w_gu, w_dn),
                     jax.ShapeDtypeStruct((bsz, seq, d), h.dtype),
                     pl.BlockSpec((bsz, tt, d), lambda s: (0, s, 0)), cast)


def _ffn(h2, g, w_gu, w_dn, cast=()):
    t, d = h2.shape
    tok = pl.BlockSpec((TM_FFN, d), lambda i: (i, 0))
    return _ffn_call(_ffn_kernel, "ffn", t // TM_FFN, (h2,), [tok], (g, w_gu, w_dn),
                     jax.ShapeDtypeStruct(h2.shape, h2.dtype), tok, cast)


def _proj_ffn(h2, attn, w_o, g, w_gu, w_dn, cast=()):
    t, d = h2.shape
    tok = pl.BlockSpec((TM_FFN, d), lambda i: (i, 0))
    return _ffn_call(_proj_ffn_kernel, "proj_ffn", t // TM_FFN, (h2, attn),
                     [tok, pl.BlockSpec((TM_FFN, attn.shape[1]), lambda i: (i, 0))],
                     (w_o, g, w_gu, w_dn), jax.ShapeDtypeStruct((t, d), h2.dtype), tok, cast)


def _row(v):
    return v.reshape(1, -1)


def _pad_lanes(v, n):
    return jnp.pad(v, [(0, 0)] * (v.ndim - 1) + [(0, n - v.shape[-1])])


def _dup(v):
    return jnp.concatenate([v, v], axis=-1)


def kernel(x, positions, mix_norm, conv_w_in, conv_w, conv_w_out, lru_w_in, lru_conv_w, lru_conv_b, lru_gate_a_w, lru_gate_a_b, lru_gate_x_w, lru_gate_x_b, lru_lambda, lru_w_out, mla_w_down, mla_q_norm, mla_kv_norm, mla_w_uq, mla_w_ukv, mla_qn_norm, mla_qr_norm, mla_kn_norm, mla_kr_norm, mla_w_o, ffn_norm, ffn_w_gu, ffn_w_down):
    bsz, seq, d = x.shape
    nh = MLA_HEADS
    half = QK_ROPE_DIM // 2

    inv_freq = ROPE_THETA ** (-jnp.arange(0, QK_ROPE_DIM, 2, dtype=F32) / QK_ROPE_DIM)
    sign_slab = _row(_pad_lanes(jnp.concatenate([-jnp.ones(half, F32), jnp.ones(half, F32)]), LANES))
    live_slab = _row(_pad_lanes(jnp.ones(QK_ROPE_DIM, F32), LANES))
    cos_tab, sin_tab = _rope_tables(positions, inv_freq, live_slab, sign_slab)
    lane_slab = jnp.arange(2 * LANES) // LANES
    pair_ones = ((lane_slab[:, None] == lane_slab[None, :]) * (1.0 / LANES)).astype(BF16)

    def mixer_jobs(i):
        kind, j = i % N_MIXERS, i // N_MIXERS
        if kind == 0:
            return ((conv_w_in, j), (conv_w_out, j))
        if kind == 1:
            return ((lru_w_in, j), (lru_w_out, j))
        return ((mla_w_o, j),)

    def ffn_jobs(i):
        return ((ffn_w_gu, i), (ffn_w_down, i))

    def cast_now(jobs):
        return tuple(stacked[index].astype(BF16) for stacked, index in jobs)

    h = x
    mixer_w = cast_now(mixer_jobs(0))
    ffn_w = None
    for i in range(DEPTH):
        kind, j = i % N_MIXERS, i // N_MIXERS
        g = _row(mix_norm[i])
        gf = _row(ffn_norm[i])
        last = i + 1 == DEPTH
        next_jobs = () if last else mixer_jobs(i + 1) + ffn_jobs(i + 1)
        to_time_major = kind == 0 and not last and (i + 1) % N_MIXERS == 1
        if ffn_w is None and not to_time_major:
            ffn_w = cast_now(ffn_jobs(i))
        if kind == 0:
            conv_args = (h, g, mixer_w[0], conv_w[j], mixer_w[1])
            if to_time_major:
                h, cast_ffn = _conv_layer_to_time_major(
                    *conv_args, cast=() if ffn_w else ffn_jobs(i))
                h, casts = _ffn(h.reshape(seq * bsz, d), gf, *(ffn_w or cast_ffn), cast=next_jobs)
                h = h.reshape(seq, bsz, d)
            else:
                h = _conv_layer(*conv_args)
                h, casts = _ffn(h.reshape(bsz * seq, d), gf, *ffn_w, cast=next_jobs)
                h = h.reshape(bsz, seq, d)
        elif kind == 1:
            wax = (0.5 * jnp.concatenate([lru_gate_a_w[j], lru_gate_x_w[j]], axis=-1)).astype(BF16)
            h = _lru_layer(h, g, mixer_w[0], lru_conv_w[j], _row(lru_conv_b[j]),
                           wax, _row(0.5 * lru_gate_a_b[j]), _row(0.5 * lru_gate_x_b[j]),
                           _row(lru_lambda[j]), mixer_w[1])
            h, casts = _ffn_from_time_major(h, gf, *ffn_w, cast=next_jobs)
        else:
            wdm = mla_w_down[j]
            wd = jnp.concatenate([wdm, wdm[:, Q_LORA_RANK + KV_LORA_RANK:]], axis=1).astype(BF16)
            wq = mla_w_uq[j].reshape(Q_LORA_RANK, nh, QK_NOPE_DIM + QK_ROPE_DIM)
            wuq = jnp.concatenate(
                [wq[:, :, :QK_NOPE_DIM].reshape(Q_LORA_RANK, nh * QK_NOPE_DIM),
                 _dup(wq[:, :, QK_NOPE_DIM:]).reshape(Q_LORA_RANK, nh * LANES)],
                axis=1).astype(BF16)
            wkv = mla_w_ukv[j].reshape(KV_LORA_RANK, nh, QK_NOPE_DIM + V_HEAD_DIM)
            wukv = jnp.concatenate(
                [wkv[:, :, :QK_NOPE_DIM].reshape(KV_LORA_RANK, nh * QK_NOPE_DIM),
                 wkv[:, :, QK_NOPE_DIM:].reshape(KV_LORA_RANK, nh * V_HEAD_DIM)], axis=1).astype(BF16)
            consts = (g, wd, _row(mla_q_norm[j]), _row(mla_kv_norm[j]), wuq, wukv,
                      _row(mla_qn_norm[j]), _row(_dup(mla_qr_norm[j])),
                      _row(mla_kn_norm[j]), _row(_dup(mla_kr_norm[j])),
                      pair_ones)
            q, k, v1 = _mla_proj(h, cos_tab, sin_tab, consts)
            attn = _attention(q, k, v1).reshape(bsz * seq, nh * V_HEAD_DIM)
            h, casts = _proj_ffn(h.reshape(bsz * seq, d), attn, mixer_w[0], gf, *ffn_w,
                                 cast=next_jobs)
            h = h.reshape(bsz, seq, d)
        if not last:
            n_mixer = len(mixer_jobs(i + 1))
            mixer_w, ffn_w = casts[:n_mixer], casts[n_mixer:]
    return h
```

```python
import math

import jax
import jax.numpy as jnp
from jax import lax
from jax.experimental import pallas as pl
from jax.experimental.pallas import tpu as pltpu

F32 = jnp.float32
BF16 = jnp.bfloat16

D_MODEL = 1024
DEPTH = 4
N_MIXERS = 3
CONV_WIDTH = 3
LRU_WIDTH = 1280
LRU_BLOCKS = 10
LRU_BLOCK_W = LRU_WIDTH // LRU_BLOCKS
LRU_CONV_WIDTH = 4
LRU_C = 8.0
MLA_HEADS = 8
Q_LORA_RANK = 384
KV_LORA_RANK = 256
QK_NOPE_DIM = 128
QK_ROPE_DIM = 64
V_HEAD_DIM = 128
ROPE_THETA = 10000.0
D_FF = 2816
NORM_EPS = 1e-6

LANES = 128
SUBLANES = 8
BF16_SUBLANES = 16
QK_DIM_PADDED = 2 * LANES
VMEM_LIMIT = 56 * 1024 * 1024

TS_CONV = 1024
CONV_SUBTILES = 2
TT_LRU = 128
LRU_SUBTILES = 2
TS_MLA = 1024
ROPE_TABLE_POS_ROWS = 32
MLA_SUBTILES = 4
TQ_ATTN = 512
ATTN_HEADS_PER_STEP = 4
ATTN_GROUPS = 4
TM_FFN = 1024
FFN_SUBTILES = 2
MXU_DIM = 256
_FFN_SPLIT = (D_FF // MXU_DIM + 1) // 2 * MXU_DIM
FFN_CHUNK_BOUNDS = ((0, _FFN_SPLIT), (_FFN_SPLIT, D_FF))
LOG2E = 1.4426950408889634


def _rms(x, g):
    return x * lax.rsqrt(jnp.mean(x * x, axis=-1, keepdims=True) + NORM_EPS) * g


def _dot(a, b):
    return jnp.dot(a, b, preferred_element_type=F32)


def _gelu_tanh(x):
    k0 = -2.0 * LOG2E * math.sqrt(2.0 / math.pi)
    return x / (1.0 + jnp.exp2(x * (k0 + (k0 * 0.044715) * (x * x))))


def _cast_side_job(jobs, n_steps):
    in_specs, out_specs, out_shapes = [], [], []
    for stacked, index in jobs:
        _, rows, cols = stacked.shape
        n_blocks = n_steps
        while rows % n_blocks or (rows // n_blocks) % BF16_SUBLANES:
            n_blocks //= 2
        rep = n_steps // n_blocks
        in_specs.append(pl.BlockSpec((None, rows // n_blocks, cols),
                                     lambda i, rep=rep, index=index: (index, i // rep, 0)))
        out_specs.append(pl.BlockSpec((rows // n_blocks, cols), lambda i, rep=rep: (i // rep, 0)))
        out_shapes.append(jax.ShapeDtypeStruct((rows, cols), BF16))
    return in_specs, out_specs, out_shapes


def _split_cast_refs(refs):
    n = (len(refs) - 1) // 2
    return refs[:n], refs[n], refs[n + 1:]


def _run_casts(srcs, dsts):
    for src, dst in zip(srcs, dsts):
        dst[...] = src[...].astype(dst.dtype)


def _const_spec(shape):
    nd = len(shape)
    return pl.BlockSpec(shape, lambda *_: (0,) * nd, pipeline_mode=pl.Buffered(1))


def _shift_rows_prev(x, prev8, d):
    r = pltpu.roll(x, d, 0)
    hd = pltpu.roll(prev8, d, 0)
    rows8 = lax.broadcasted_iota(jnp.int32, prev8.shape, 0)
    first = jnp.where(rows8 < d, hd, r[:SUBLANES])
    return jnp.concatenate([first, r[SUBLANES:]], axis=0)


def _conv_layer_kernel(h_ref, g_ref, win_ref, cw_ref, wout_ref, o_ref, carry_ref):
    @pl.when(pl.program_id(1) == 0)
    def _():
        carry_ref[...] = jnp.zeros_like(carry_ref)

    rows = h_ref.shape[1] // CONV_SUBTILES
    cw = cw_ref[...]

    def project_in(sub):
        h = h_ref[0, sub * rows:(sub + 1) * rows, :]
        return h, _dot(_rms(h, g_ref[...]).astype(BF16), win_ref[...])

    prev8 = carry_ref[...]
    projected = project_in(0)
    for sub in range(CONV_SUBTILES):
        h, bch = projected
        if sub + 1 < CONV_SUBTILES:
            projected = project_in(sub + 1)
        b_gate = bch[:, :D_MODEL]
        u = bch[:, D_MODEL:2 * D_MODEL] * bch[:, 2 * D_MODEL:]
        conv = (cw[0:1] * _shift_rows_prev(u, prev8, 2)
                + cw[1:2] * _shift_rows_prev(u, prev8, 1)
                + cw[2:3] * u)
        prev8 = u[rows - SUBLANES:]
        y = _dot((b_gate * conv).astype(BF16), wout_ref[...])
        o_ref[0, sub * rows:(sub + 1) * rows, :] = h + y
    carry_ref[...] = prev8


def _conv_layer(h, g, w_in, cw, w_out):
    bsz, seq, d = h.shape
    ts = TS_CONV
    tok = pl.BlockSpec((1, ts, d), lambda b, s: (b, s, 0))
    return pl.pallas_call(
        _conv_layer_kernel,
        out_shape=jax.ShapeDtypeStruct(h.shape, h.dtype),
        grid=(bsz, seq // ts),
        in_specs=[
            tok, _const_spec(g.shape), _const_spec(w_in.shape), _const_spec(cw.shape),
            _const_spec(w_out.shape),
        ],
        out_specs=tok,
        scratch_shapes=[pltpu.VMEM((SUBLANES, d), F32)],
        compiler_params=pltpu.CompilerParams(
            dimension_semantics=("arbitrary", "arbitrary"), vmem_limit_bytes=VMEM_LIMIT),
        name="conv_layer",
    )(h, g, w_in, cw, w_out)


def _conv_layer_tm_kernel(h_ref, g_ref, win_ref, cw_ref, wout_ref, *rest):
    cast_srcs, o_ref, cast_dsts = _split_cast_refs(rest[:-1])
    carry_ref = rest[-1]
    _run_casts(cast_srcs, cast_dsts)

    @pl.when(pl.program_id(0) == 0)
    def _():
        carry_ref[...] = jnp.zeros_like(carry_ref)

    nb, tt, d = h_ref.shape
    ts = tt // CONV_SUBTILES
    rows = ts * nb
    halo = (CONV_WIDTH - 1) * nb
    cw = cw_ref[...]

    def project_in(sub):
        h = jnp.swapaxes(h_ref[:, sub * ts:(sub + 1) * ts, :], 0, 1).reshape(rows, d)
        return h, _dot(_rms(h, g_ref[...]).astype(BF16), win_ref[...])

    u_prev = carry_ref[...]
    projected = project_in(0)
    for sub in range(CONV_SUBTILES):
        h, bch = projected
        if sub + 1 < CONV_SUBTILES:
            projected = project_in(sub + 1)
        b_gate = bch[:, :D_MODEL]
        u = bch[:, D_MODEL:2 * D_MODEL] * bch[:, 2 * D_MODEL:]
        ext = jnp.concatenate([u_prev, u], axis=0)
        u_prev = u[rows - halo:]
        conv = cw[CONV_WIDTH - 1:CONV_WIDTH] * u
        for k in range(CONV_WIDTH - 1):
            conv = conv + cw[k:k + 1] * ext[k * nb:k * nb + rows]
        y = _dot((b_gate * conv).astype(BF16), wout_ref[...])
        o_ref[sub * ts:(sub + 1) * ts] = (h + y).reshape(ts, nb, d)
    carry_ref[...] = u_prev


def _conv_layer_to_time_major(h, g, w_in, cw, w_out, cast=()):
    bsz, seq, d = h.shape
    assert bsz == SUBLANES
    tt = TS_CONV // bsz
    cast_in, cast_out, cast_shapes = _cast_side_job(cast, seq // tt)
    out = pl.pallas_call(
        _conv_layer_tm_kernel,
        out_shape=[jax.ShapeDtypeStruct((seq, bsz, d), h.dtype)] + cast_shapes,
        grid=(seq // tt,),
        in_specs=[
            pl.BlockSpec((bsz, tt, d), lambda s: (0, s, 0)),
            _const_spec(g.shape), _const_spec(w_in.shape), _const_spec(cw.shape),
            _const_spec(w_out.shape),
        ] + cast_in,
        out_specs=[pl.BlockSpec((tt, bsz, d), lambda s: (s, 0, 0))] + cast_out,
        scratch_shapes=[pltpu.VMEM(((CONV_WIDTH - 1) * bsz, d), F32)],
        compiler_params=pltpu.CompilerParams(
            dimension_semantics=("arbitrary",), vmem_limit_bytes=VMEM_LIMIT),
        name="conv_layer_tm",
    )(h, g, w_in, cw, w_out, *[stacked for stacked, _ in cast])
    return out[0], tuple(out[1:])


def _lru_layer_kernel(h_ref, g_ref, win_ref, cw_ref, cb_ref, wax_ref, ba_ref, bx_ref, lam_ref,
                      wout_ref, o_ref, xcarry_ref, hcarry_ref):
    @pl.when(pl.program_id(0) == 0)
    def _():
        xcarry_ref[...] = jnp.zeros_like(xcarry_ref)
        hcarry_ref[...] = jnp.zeros_like(hcarry_ref)

    tt, nb, d = h_ref.shape
    ts = tt // LRU_SUBTILES
    rows = ts * nb
    halo = (LRU_CONV_WIDTH - 1) * nb
    cw = cw_ref[...]
    lam = lam_ref[...]
    log_sig_lam = jnp.minimum(lam, 0.0) - jnp.log1p(jnp.exp(-jnp.abs(lam)))
    neg_half_c_lsl = (-0.5 * LRU_C) * log_sig_lam
    ba = ba_ref[...]
    bx = bx_ref[...]
    x_prev = xcarry_ref[...]
    h_prev = hcarry_ref[...]

    def project_in(sub):
        h = h_ref[sub * ts:(sub + 1) * ts].reshape(rows, d)
        xn = _rms(h, g_ref[...]).astype(BF16)
        return h, _dot(xn, win_ref[...])

    projected = project_in(0)
    for sub in range(LRU_SUBTILES):
        t0 = sub * ts
        h, gr = projected
        if sub + 1 < LRU_SUBTILES:
            projected = project_in(sub + 1)
        gate = _gelu_tanh(gr[:, :LRU_WIDTH])
        xr = gr[:, LRU_WIDTH:]
        ext = jnp.concatenate([x_prev, xr], axis=0)
        x_prev = xr[rows - halo:]
        rec = cb_ref[...] + cw[LRU_CONV_WIDTH - 1:LRU_CONV_WIDTH] * xr
        for k in range(LRU_CONV_WIDTH - 1):
            rec = rec + cw[k:k + 1] * ext[k * nb:k * nb + rows]

        outs = []
        for n in range(LRU_BLOCKS):
            sl = slice(n * LRU_BLOCK_W, (n + 1) * LRU_BLOCK_W)
            xb = rec[:, sl]
            ri = _dot(xb.astype(BF16), wax_ref[n])
            neg_log_a = neg_half_c_lsl[:, sl] * (jnp.tanh(ri[:, :LRU_BLOCK_W] + ba[:, sl]) + 1.0)
            i = 0.5 * jnp.tanh(ri[:, LRU_BLOCK_W:] + bx[:, sl]) + 0.5
            a = jnp.exp2(neg_log_a * (-LOG2E))
            z = jnp.tanh(neg_log_a) * (a * a + 1.0)
            mult = jnp.where(z > 0.0, z * lax.rsqrt(z), 0.0)
            b = mult * (i * xb)
            hp = h_prev[:, sl]
            steps = []
            for t in range(ts):
                hp = a[t * nb:(t + 1) * nb] * hp + b[t * nb:(t + 1) * nb]
                steps.append(hp)
            outs.append(jnp.concatenate(steps, axis=0))
        hs = jnp.concatenate(outs, axis=1)
        h_prev = hs[rows - nb:]
        y = _dot((gate * hs).astype(BF16), wout_ref[...])
        o_ref[t0:t0 + ts] = (h + y).reshape(ts, nb, d)

    xcarry_ref[...] = x_prev
    hcarry_ref[...] = h_prev


def _lru_layer(h, g, w_in, cw, cb, wax, ba, bx, lam, w_out):
    seq, bsz, d = h.shape
    assert bsz == SUBLANES
    tt = TT_LRU
    consts = (g, w_in, cw, cb, wax, ba, bx, lam, w_out)
    tok = pl.BlockSpec((tt, bsz, d), lambda s: (s, 0, 0))
    return pl.pallas_call(
        _lru_layer_kernel,
        out_shape=jax.ShapeDtypeStruct(h.shape, h.dtype),
        grid=(seq // tt,),
        in_specs=[tok] + [_const_spec(c.shape) for c in consts],
        out_specs=tok,
        scratch_shapes=[pltpu.VMEM(((LRU_CONV_WIDTH - 1) * bsz, LRU_WIDTH), F32),
                        pltpu.VMEM((bsz, LRU_WIDTH), F32)],
        compiler_params=pltpu.CompilerParams(
            dimension_semantics=("arbitrary",), vmem_limit_bytes=VMEM_LIMIT),
        name="lru_layer",
    )(h, *consts)


def _rope_table_kernel(pos_ref, freq_ref, live_ref, sign_ref, cos_ref, sin_ref):
    for r in range(pos_ref.shape[0]):
        ang = freq_ref[...] * pos_ref[r:r + 1, :]
        reps = LANES // ang.shape[0]
        cos_t = jnp.concatenate([jnp.cos(ang)] * reps, axis=0).T
        sin_t = jnp.concatenate([jnp.sin(ang)] * reps, axis=0).T
        cos_ref[r * LANES:(r + 1) * LANES, :] = cos_t * live_ref[...]
        sin_ref[r * LANES:(r + 1) * LANES, :] = sin_t * sign_ref[...]


def _rope_tables(positions, inv_freq, live_slab, sign_slab):
    bsz, seq = positions.shape
    n = bsz * seq
    pos2d = positions.astype(F32).reshape(n // LANES, LANES)
    rows = ROPE_TABLE_POS_ROWS * LANES
    table = jax.ShapeDtypeStruct((n, LANES), F32)
    small = lambda shape: pl.BlockSpec(shape, lambda i: (0, 0))
    cos, sin = pl.pallas_call(
        _rope_table_kernel,
        out_shape=[table, table],
        grid=(n // rows,),
        in_specs=[pl.BlockSpec((ROPE_TABLE_POS_ROWS, LANES), lambda i: (i, 0)),
                  small((inv_freq.shape[0], 1)), small((1, LANES)), small((1, LANES))],
        out_specs=[pl.BlockSpec((rows, LANES), lambda i: (i, 0))] * 2,
        name="rope_tables",
    )(pos2d, inv_freq.reshape(-1, 1), live_slab, sign_slab)
    return cos.reshape(bsz, seq, LANES), sin.reshape(bsz, seq, LANES)


def _mla_proj_kernel(h_ref, cos_ref, sin_ref, g_ref, wd_ref, qg_ref, kvg_ref, wuq_ref, wukv_ref,
                     qng_ref, qrg_ref, kng_ref, krg_ref, pair_ref,
                     q_ref, k_ref, v_ref):
    ts = h_ref.shape[1] // MLA_SUBTILES
    nope_all = MLA_HEADS * QK_NOPE_DIM
    half = QK_ROPE_DIM // 2
    q_scale = LOG2E / math.sqrt(QK_NOPE_DIM + QK_ROPE_DIM)
    qn_gain = qng_ref[...] * q_scale
    qr_gain = qrg_ref[...] * q_scale
    ones = jnp.ones((ts, LANES), BF16)

    def slab_norm_scales(x):
        scales = []
        for p in range(x.shape[1] // (2 * LANES)):
            xs = x[:, p * 2 * LANES:(p + 1) * 2 * LANES]
            r = lax.rsqrt(_dot((xs * xs).astype(BF16), pair_ref[...]) + NORM_EPS)
            scales += [r[:, :LANES], r[:, LANES:]]
        return scales

    def project(sub):
        rows = pl.ds(sub * ts, ts)
        xn = _rms(h_ref[0, rows, :], g_ref[...]).astype(BF16)
        c = _dot(xn, wd_ref[...])
        c_q = c[:, :Q_LORA_RANK]
        c_kv = c[:, Q_LORA_RANK:Q_LORA_RANK + KV_LORA_RANK]
        k_rope = c[:, Q_LORA_RANK + KV_LORA_RANK:]
        q = _dot(_rms(c_q, qg_ref[...]).astype(BF16), wuq_ref[...])
        kv = _dot(_rms(c_kv, kvg_ref[...]).astype(BF16), wukv_ref[...])
        rq_nope = slab_norm_scales(q[:, :nope_all])
        rq_rope = slab_norm_scales(q[:, nope_all:])
        rk_nope = slab_norm_scales(kv[:, :nope_all])
        rk_rope = lax.rsqrt(_dot((k_rope * k_rope).astype(BF16), pair_ref[:LANES, :LANES]) + NORM_EPS)
        return q, kv, k_rope, rq_nope, rq_rope, rk_nope, rk_rope

    def finish(sub, projected):
        q, kv, k_rope, rq_nope, rq_rope, rk_nope, rk_rope = projected
        rows = pl.ds(sub * ts, ts)
        cos = cos_ref[0, rows, :]
        sin = sin_ref[0, rows, :]

        def rope(y):
            return y * cos + pltpu.roll(y, half, 1) * sin

        k_rope = rope(k_rope * rk_rope * krg_ref[...]).astype(BF16)
        for hh in range(MLA_HEADS):
            sl = slice(hh * LANES, (hh + 1) * LANES)
            sl2 = slice(nope_all + hh * LANES, nope_all + (hh + 1) * LANES)
            q_ref[0, hh, rows, :LANES] = (q[:, sl] * rq_nope[hh] * qn_gain).astype(BF16)
            q_ref[0, hh, rows, LANES:] = rope(q[:, sl2] * rq_rope[hh] * qr_gain).astype(BF16)
            k_ref[0, hh, rows, :LANES] = (kv[:, sl] * rk_nope[hh] * kng_ref[...]).astype(BF16)
            k_ref[0, hh, rows, LANES:] = k_rope
            v_ref[0, hh, rows, :LANES] = kv[:, sl2].astype(BF16)
            v_ref[0, hh, rows, LANES:] = ones

    projected = project(0)
    for sub in range(MLA_SUBTILES):
        current = projected
        if sub + 1 < MLA_SUBTILES:
            projected = project(sub + 1)
        finish(sub, current)


def _mla_proj(h, cos_tab, sin_tab, consts):
    bsz, seq, d = h.shape
    ts = TS_MLA
    out_sds = jax.ShapeDtypeStruct((bsz, MLA_HEADS, seq, QK_DIM_PADDED), BF16)
    out_spec = pl.BlockSpec((1, MLA_HEADS, ts, QK_DIM_PADDED), lambda b, s: (b, 0, s, 0))
    slab_spec = pl.BlockSpec((1, ts, LANES), lambda b, s: (b, s, 0))
    return pl.pallas_call(
        _mla_proj_kernel,
        out_shape=(out_sds, out_sds, out_sds),
        grid=(bsz, seq // ts),
        in_specs=[pl.BlockSpec((1, ts, d), lambda b, s: (b, s, 0)), slab_spec, slab_spec]
        + [_const_spec(c.shape) for c in consts],
        out_specs=(out_spec, out_spec, out_spec),
        compiler_params=pltpu.CompilerParams(
            dimension_semantics=("arbitrary", "arbitrary"), vmem_limit_bytes=VMEM_LIMIT),
        name="mla_proj",
    )(h, cos_tab, sin_tab, *consts)


def _attn_kernel(q_ref, k_ref, v_ref, o_ref, m_ref, acc_ref, *s_refs):
    qi = pl.program_id(2)
    nhp, tq = q_ref.shape[1], q_ref.shape[2]
    tk = tq
    n_groups = len(s_refs)
    ng = nhp // n_groups
    m_ref[...] = jnp.full_like(m_ref, -jnp.inf)
    acc_ref[...] = jnp.zeros_like(acc_ref)

    def logits(kb, group):
        start = pl.multiple_of(kb * tk, tk)
        for i in range(ng):
            hp = group * ng + i
            k = k_ref[0, hp, pl.ds(start, tk), :]
            s_refs[group][i] = lax.dot_general(q_ref[0, hp], k, (((1,), (1,)), ((), ())),
                                               preferred_element_type=F32)

    def softmax_pv(kb, group, masked):
        start = pl.multiple_of(kb * tk, tk)
        for i in range(ng):
            hp = group * ng + i
            s = s_refs[group][i]
            if masked:
                row = lax.broadcasted_iota(jnp.int32, s.shape, 0)
                col = lax.broadcasted_iota(jnp.int32, s.shape, 1)
                s = jnp.where(col <= row, s, jnp.finfo(F32).min)
            m_prev = m_ref[hp]
            m_next = jnp.maximum(m_prev, jnp.max(s, axis=1, keepdims=True))
            alpha = jnp.exp2(m_prev - m_next)
            p = jnp.exp2(s - jnp.concatenate([m_next] * (tk // LANES), axis=1))
            v = v_ref[0, hp, pl.ds(start, tk), :]
            acc_ref[hp] = (jnp.concatenate([alpha, alpha], axis=1) * acc_ref[hp]
                           + _dot(p.astype(BF16), v))
            m_ref[hp] = m_next

    logits(0, 0)

    def unmasked_block(kb):
        for group in range(n_groups):
            if group + 1 < n_groups:
                logits(kb, group + 1)
            else:
                logits(kb + 1, 0)
            softmax_pv(kb, group, False)

    odd = qi % 2

    @pl.when(odd == 1)
    def _():
        unmasked_block(0)

    def body(j, carry):
        unmasked_block(odd + 2 * j)
        unmasked_block(odd + 2 * j + 1)
        return carry

    lax.fori_loop(0, qi // 2, body, 0)
    for group in range(n_groups):
        if group + 1 < n_groups:
            logits(qi, group + 1)
        softmax_pv(qi, group, True)

    for hp in range(nhp):
        acc = acc_ref[hp]
        o_ref[0, :, hp * LANES:(hp + 1) * LANES] = (acc[:, :LANES] / acc[:, LANES:]).astype(o_ref.dtype)


def _attention(q, k, v1):
    bsz, nh, seq, dq = q.shape
    tq = TQ_ATTN
    hp = ATTN_HEADS_PER_STEP
    return pl.pallas_call(
        _attn_kernel,
        out_shape=jax.ShapeDtypeStruct((bsz, seq, nh * V_HEAD_DIM), BF16),
        grid=(bsz, nh // hp, seq // tq),
        in_specs=[pl.BlockSpec((1, hp, tq, dq), lambda b, h, i: (b, h, i, 0)),
                  pl.BlockSpec((1, hp, seq, dq), lambda b, h, i: (b, h, 0, 0)),
                  pl.BlockSpec((1, hp, seq, dq), lambda b, h, i: (b, h, 0, 0))],
        out_specs=pl.BlockSpec((1, tq, hp * V_HEAD_DIM), lambda b, h, i: (b, i, h)),
        scratch_shapes=[pltpu.VMEM((hp, tq, LANES), F32), pltpu.VMEM((hp, tq, dq), F32)]
        + [pltpu.VMEM((hp // ATTN_GROUPS, tq, tq), F32)] * ATTN_GROUPS,
        compiler_params=pltpu.CompilerParams(
            dimension_semantics=("arbitrary", "arbitrary", "arbitrary"),
            vmem_limit_bytes=VMEM_LIMIT),
        name="mla_attention",
    )(q, k, v1)


def _ffn_subtiles(load, store, g_ref, wgu_ref, wdn_ref):
    def prepare(i):
        h = load(i)
        return h, _rms(h, g_ref[...]).astype(BF16)

    prepared = prepare(0)
    for i in range(FFN_SUBTILES):
        h, xn = prepared
        out = h
        for ci, (lo, hi) in enumerate(FFN_CHUNK_BOUNDS):
            gate = _dot(xn, wgu_ref[:, lo:hi])
            up = _dot(xn, wgu_ref[:, D_FF + lo:D_FF + hi])
            if ci == 0 and i + 1 < FFN_SUBTILES:
                prepared = prepare(i + 1)
            act = (jax.nn.silu(gate) * up).astype(BF16)
            out = out + _dot(act, wdn_ref[lo:hi, :])
        store(i, out)


def _ffn_kernel(h_ref, g_ref, wgu_ref, wdn_ref, *rest):
    cast_srcs, o_ref, cast_dsts = _split_cast_refs(rest)
    _run_casts(cast_srcs, cast_dsts)
    rows = h_ref.shape[0] // FFN_SUBTILES

    def store(i, out):
        o_ref[i * rows:(i + 1) * rows] = out

    _ffn_subtiles(lambda i: h_ref[i * rows:(i + 1) * rows], store, g_ref, wgu_ref, wdn_ref)


def _ffn_from_time_major_kernel(h_ref, g_ref, wgu_ref, wdn_ref, *rest):
    cast_srcs, o_ref, cast_dsts = _split_cast_refs(rest)
    _run_casts(cast_srcs, cast_dsts)
    tt, nb, d = h_ref.shape
    ts = tt // FFN_SUBTILES

    def store(i, out):
        o_ref[:, i * ts:(i + 1) * ts, :] = jnp.swapaxes(out.reshape(ts, nb, d), 0, 1)

    _ffn_subtiles(lambda i: h_ref[i * ts:(i + 1) * ts].reshape(ts * nb, d), store,
                  g_ref, wgu_ref, wdn_ref)


def _proj_ffn_kernel(h_ref, a_ref, wo_ref, g_ref, wgu_ref, wdn_ref, *rest):
    cast_srcs, o_ref, cast_dsts = _split_cast_refs(rest)
    _run_casts(cast_srcs, cast_dsts)
    rows = h_ref.shape[0] // FFN_SUBTILES

    def load(i):
        sl = slice(i * rows, (i + 1) * rows)
        return h_ref[sl] + _dot(a_ref[sl], wo_ref[...])

    def store(i, out):
        o_ref[i * rows:(i + 1) * rows] = out

    _ffn_subtiles(load, store, g_ref, wgu_ref, wdn_ref)


def _ffn_call(kernel, name, n_steps, tokens, token_specs, consts, out_sds, out_spec, cast):
    cast_in, cast_out, cast_shapes = _cast_side_job(cast, n_steps)
    out = pl.pallas_call(
        kernel,
        out_shape=[out_sds] + cast_shapes,
        grid=(n_steps,),
        in_specs=list(token_specs) + [_const_spec(c.shape) for c in consts] + cast_in,
        out_specs=[out_spec] + cast_out,
        compiler_params=pltpu.CompilerParams(
            dimension_semantics=("arbitrary",), vmem_limit_bytes=VMEM_LIMIT),
        name=name,
    )(*tokens, *consts, *[stacked for stacked, _ in cast])
    return out[0], tuple(out[1:])


def _ffn_from_time_major(h, g, w_gu, w_dn, cast=()):
    seq, bsz, d = h.shape
    tt = TM_FFN // bsz
    return _ffn_call(_ffn_from_time_major_kernel, "ffn_tm", seq // tt, (h,),
                     [pl.BlockSpec((tt, bsz, d), lambda s: (s, 0, 0))], (g, w_gu, w_dn),
                     jax.ShapeDtypeStruct((bsz, seq, d), h.dtype),
                     pl.BlockSpec((bsz, tt, d), lambda s: (0, s, 0)), cast)


def _ffn(h2, g, w_gu, w_dn, cast=()):
    t, d = h2.shape
    tok = pl.BlockSpec((TM_FFN, d), lambda i: (i, 0))
    return _ffn_call(_ffn_kernel, "ffn", t // TM_FFN, (h2,), [tok], (g, w_gu, w_dn),
                     jax.ShapeDtypeStruct(h2.shape, h2.dtype), tok, cast)


def _proj_ffn(h2, attn, w_o, g, w_gu, w_dn, cast=()):
    t, d = h2.shape
    tok = pl.BlockSpec((TM_FFN, d), lambda i: (i, 0))
    return _ffn_call(_proj_ffn_kernel, "proj_ffn", t // TM_FFN, (h2, attn),
                     [tok, pl.BlockSpec((TM_FFN, attn.shape[1]), lambda i: (i, 0))],
                     (w_o, g, w_gu, w_dn), jax.ShapeDtypeStruct((t, d), h2.dtype), tok, cast)


def _row(v):
    return v.reshape(1, -1)


def _pad_lanes(v, n):
    return jnp.pad(v, [(0, 0)] * (v.ndim - 1) + [(0, n - v.shape[-1])])


def _dup(v):
    return jnp.concatenate([v, v], axis=-1)


def kernel(x, positions, mix_norm, conv_w_in, conv_w, conv_w_out, lru_w_in, lru_conv_w, lru_conv_b, lru_gate_a_w, lru_gate_a_b, lru_gate_x_w, lru_gate_x_b, lru_lambda, lru_w_out, mla_w_down, mla_q_norm, mla_kv_norm, mla_w_uq, mla_w_ukv, mla_qn_norm, mla_qr_norm, mla_kn_norm, mla_kr_norm, mla_w_o, ffn_norm, ffn_w_gu, ffn_w_down):
    bsz, seq, d = x.shape
    nh = MLA_HEADS
    half = QK_ROPE_DIM // 2

    inv_freq = ROPE_THETA ** (-jnp.arange(0, QK_ROPE_DIM, 2, dtype=F32) / QK_ROPE_DIM)
    sign_slab = _row(_pad_lanes(jnp.concatenate([-jnp.ones(half, F32), jnp.ones(half, F32)]), LANES))
    live_slab = _row(_pad_lanes(jnp.ones(QK_ROPE_DIM, F32), LANES))
    cos_tab, sin_tab = _rope_tables(positions, inv_freq, live_slab, sign_slab)
    lane_slab = jnp.arange(2 * LANES) // LANES
    pair_ones = ((lane_slab[:, None] == lane_slab[None, :]) * (1.0 / LANES)).astype(BF16)

    def mixer_jobs(i):
        kind, j = i % N_MIXERS, i // N_MIXERS
        if kind == 0:
            return ((conv_w_in, j), (conv_w_out, j))
        if kind == 1:
            return ((lru_w_in, j), (lru_w_out, j))
        return ((mla_w_o, j),)

    def ffn_jobs(i):
        return ((ffn_w_gu, i), (ffn_w_down, i))

    def cast_now(jobs):
        return tuple(stacked[index].astype(BF16) for stacked, index in jobs)

    h = x
    mixer_w = cast_now(mixer_jobs(0))
    ffn_w = None
    for i in range(DEPTH):
        kind, j = i % N_MIXERS, i // N_MIXERS
        g = _row(mix_norm[i])
        gf = _row(ffn_norm[i])
        last = i + 1 == DEPTH
        next_jobs = () if last else mixer_jobs(i + 1) + ffn_jobs(i + 1)
        to_time_major = kind == 0 and not last and (i + 1) % N_MIXERS == 1
        if ffn_w is None and not to_time_major:
            ffn_w = cast_now(ffn_jobs(i))
        if kind == 0:
            conv_args = (h, g, mixer_w[0], conv_w[j], mixer_w[1])
            if to_time_major:
                h, cast_ffn = _conv_layer_to_time_major(
                    *conv_args, cast=() if ffn_w else ffn_jobs(i))
                h, casts = _ffn(h.reshape(seq * bsz, d), gf, *(ffn_w or cast_ffn), cast=next_jobs)
                h = h.reshape(seq, bsz, d)
            else:
                h = _conv_layer(*conv_args)
                h, casts = _ffn(h.reshape(bsz * seq, d), gf, *ffn_w, cast=next_jobs)
                h = h.reshape(bsz, seq, d)
        elif kind == 1:
            wax = (0.5 * jnp.concatenate([lru_gate_a_w[j], lru_gate_x_w[j]], axis=-1)).astype(BF16)
            h = _lru_layer(h, g, mixer_w[0], lru_conv_w[j], _row(lru_conv_b[j]),
                           wax, _row(0.5 * lru_gate_a_b[j]), _row(0.5 * lru_gate_x_b[j]),
                           _row(lru_lambda[j]), mixer_w[1])
            h, casts = _ffn_from_time_major(h, gf, *ffn_w, cast=next_jobs)
        else:
            wdm = mla_w_down[j]
            wd = jnp.concatenate([wdm, wdm[:, Q_LORA_RANK + KV_LORA_RANK:]], axis=1).astype(BF16)
            wq = mla_w_uq[j].reshape(Q_LORA_RANK, nh, QK_NOPE_DIM + QK_ROPE_DIM)
            wuq = jnp.concatenate(
                [wq[:, :, :QK_NOPE_DIM].reshape(Q_LORA_RANK, nh * QK_NOPE_DIM),
                 _dup(wq[:, :, QK_NOPE_DIM:]).reshape(Q_LORA_RANK, nh * LANES)],
                axis=1).astype(BF16)
            wkv = mla_w_ukv[j].reshape(KV_LORA_RANK, nh, QK_NOPE_DIM + V_HEAD_DIM)
            wukv = jnp.concatenate(
                [wkv[:, :, :QK_NOPE_DIM].reshape(KV_LORA_RANK, nh * QK_NOPE_DIM),
                 wkv[:, :, QK_NOPE_DIM:].reshape(KV_LORA_RANK, nh * V_HEAD_DIM)], axis=1).astype(BF16)
            consts = (g, wd, _row(mla_q_norm[j]), _row(mla_kv_norm[j]), wuq, wukv,
                      _row(mla_qn_norm[j]), _row(_dup(mla_qr_norm[j])),
                      _row(mla_kn_norm[j]), _row(_dup(mla_kr_norm[j])),
                      pair_ones)
            q, k, v1 = _mla_proj(h, cos_tab, sin_tab, consts)
            attn = _attention(q, k, v1).reshape(bsz * seq, nh * V_HEAD_DIM)
            h, casts = _proj_ffn(h.reshape(bsz * seq, d), attn, mixer_w[0], gf, *ffn_w,
                                 cast=next_jobs)
            h = h.reshape(bsz, seq, d)
        if not last:
            n_mixer = len(mixer_jobs(i + 1))
            mixer_w, ffn_w = casts[:n_mixer], casts[n_mixer:]
    return h
```

```python
import math

import jax
import jax.numpy as jnp
from jax import lax
from jax.experimental import pallas as pl
from jax.experimental.pallas import tpu as pltpu

F32 = jnp.float32
BF16 = jnp.bfloat16

D_MODEL = 1024
DEPTH = 4
N_MIXERS = 3
CONV_WIDTH = 3
LRU_WIDTH = 1280
LRU_BLOCKS = 10
LRU_BLOCK_W = LRU_WIDTH // LRU_BLOCKS
LRU_CONV_WIDTH = 4
LRU_C = 8.0
MLA_HEADS = 8
Q_LORA_RANK = 384
KV_LORA_RANK = 256
QK_NOPE_DIM = 128
QK_ROPE_DIM = 64
V_HEAD_DIM = 128
ROPE_THETA = 10000.0
D_FF = 2816
NORM_EPS = 1e-6

LANES = 128
SUBLANES = 8
BF16_SUBLANES = 16
QK_DIM_PADDED = 2 * LANES
VMEM_LIMIT = 56 * 1024 * 1024

TS_CONV = 1024
CONV_SUBTILES = 4
TT_LRU = 128
LRU_SUBTILES = 2
TS_MLA = 1024
ROPE_TABLE_POS_ROWS = 32
MLA_SUBTILES = 4
TQ_ATTN = 512
ATTN_HEADS_PER_STEP = 4
ATTN_GROUPS = 4
TM_FFN = 1024
FFN_SUBTILES = 4
MXU_DIM = 256
_FFN_SPLIT = (D_FF // MXU_DIM + 1) // 2 * MXU_DIM
FFN_CHUNK_BOUNDS = ((0, _FFN_SPLIT), (_FFN_SPLIT, D_FF))
LOG2E = 1.4426950408889634


def _rms(x, g):
    return x * lax.rsqrt(jnp.mean(x * x, axis=-1, keepdims=True) + NORM_EPS) * g


def _dot(a, b):
    return jnp.dot(a, b, preferred_element_type=F32)


def _gelu_tanh(x):
    k0 = -2.0 * LOG2E * math.sqrt(2.0 / math.pi)
    return x / (1.0 + jnp.exp2(x * (k0 + (k0 * 0.044715) * (x * x))))


def _cast_side_job(jobs, n_steps):
    in_specs, out_specs, out_shapes = [], [], []
    for stacked, index in jobs:
        _, rows, cols = stacked.shape
        n_blocks = n_steps
        while rows % n_blocks or (rows // n_blocks) % BF16_SUBLANES:
            n_blocks //= 2
        rep = n_steps // n_blocks
        in_specs.append(pl.BlockSpec((None, rows // n_blocks, cols),
                                     lambda i, rep=rep, index=index: (index, i // rep, 0)))
        out_specs.append(pl.BlockSpec((rows // n_blocks, cols), lambda i, rep=rep: (i // rep, 0)))
        out_shapes.append(jax.ShapeDtypeStruct((rows, cols), BF16))
    return in_specs, out_specs, out_shapes


def _split_cast_refs(refs):
    n = (len(refs) - 1) // 2
    return refs[:n], refs[n], refs[n + 1:]


def _run_casts(srcs, dsts):
    for src, dst in zip(srcs, dsts):
        dst[...] = src[...].astype(dst.dtype)


def _const_spec(shape):
    nd = len(shape)
    return pl.BlockSpec(shape, lambda *_: (0,) * nd, pipeline_mode=pl.Buffered(1))


def _shift_rows_prev(x, prev8, d):
    r = pltpu.roll(x, d, 0)
    hd = pltpu.roll(prev8, d, 0)
    rows8 = lax.broadcasted_iota(jnp.int32, prev8.shape, 0)
    first = jnp.where(rows8 < d, hd, r[:SUBLANES])
    return jnp.concatenate([first, r[SUBLANES:]], axis=0)


def _conv_layer_kernel(h_ref, g_ref, win_ref, cw_ref, wout_ref, o_ref, carry_ref):
    @pl.when(pl.program_id(1) == 0)
    def _():
        carry_ref[...] = jnp.zeros_like(carry_ref)

    rows = h_ref.shape[1] // CONV_SUBTILES
    cw = cw_ref[...]

    def project_in(sub):
        h = h_ref[0, sub * rows:(sub + 1) * rows, :]
        return h, _dot(_rms(h, g_ref[...]).astype(BF16), win_ref[...])

    prev8 = carry_ref[...]
    projected = project_in(0)
    for sub in range(CONV_SUBTILES):
        h, bch = projected
        if sub + 1 < CONV_SUBTILES:
            projected = project_in(sub + 1)
        b_gate = bch[:, :D_MODEL]
        u = bch[:, D_MODEL:2 * D_MODEL] * bch[:, 2 * D_MODEL:]
        conv = (cw[0:1] * _shift_rows_prev(u, prev8, 2)
                + cw[1:2] * _shift_rows_prev(u, prev8, 1)
                + cw[2:3] * u)
        prev8 = u[rows - SUBLANES:]
        y = _dot((b_gate * conv).astype(BF16), wout_ref[...])
        o_ref[0, sub * rows:(sub + 1) * rows, :] = h + y
    carry_ref[...] = prev8


def _conv_layer(h, g, w_in, cw, w_out):
    bsz, seq, d = h.shape
    ts = TS_CONV
    tok = pl.BlockSpec((1, ts, d), lambda b, s: (b, s, 0))
    return pl.pallas_call(
        _conv_layer_kernel,
        out_shape=jax.ShapeDtypeStruct(h.shape, h.dtype),
        grid=(bsz, seq // ts),
        in_specs=[
            tok, _const_spec(g.shape), _const_spec(w_in.shape), _const_spec(cw.shape),
            _const_spec(w_out.shape),
        ],
        out_specs=tok,
        scratch_shapes=[pltpu.VMEM((SUBLANES, d), F32)],
        compiler_params=pltpu.CompilerParams(
            dimension_semantics=("arbitrary", "arbitrary"), vmem_limit_bytes=VMEM_LIMIT),
        name="conv_layer",
    )(h, g, w_in, cw, w_out)


def _conv_layer_tm_kernel(h_ref, g_ref, win_ref, cw_ref, wout_ref, *rest):
    cast_srcs, o_ref, cast_dsts = _split_cast_refs(rest[:-1])
    carry_ref = rest[-1]
    _run_casts(cast_srcs, cast_dsts)

    @pl.when(pl.program_id(0) == 0)
    def _():
        carry_ref[...] = jnp.zeros_like(carry_ref)

    nb, tt, d = h_ref.shape
    ts = tt // CONV_SUBTILES
    rows = ts * nb
    halo = (CONV_WIDTH - 1) * nb
    cw = cw_ref[...]

    def project_in(sub):
        h = jnp.swapaxes(h_ref[:, sub * ts:(sub + 1) * ts, :], 0, 1).reshape(rows, d)
        return h, _dot(_rms(h, g_ref[...]).astype(BF16), win_ref[...])

    u_prev = carry_ref[...]
    projected = project_in(0)
    for sub in range(CONV_SUBTILES):
        h, bch = projected
        if sub + 1 < CONV_SUBTILES:
            projected = project_in(sub + 1)
        b_gate = bch[:, :D_MODEL]
        u = bch[:, D_MODEL:2 * D_MODEL] * bch[:, 2 * D_MODEL:]
        ext = jnp.concatenate([u_prev, u], axis=0)
        u_prev = u[rows - halo:]
        conv = cw[CONV_WIDTH - 1:CONV_WIDTH] * u
        for k in range(CONV_WIDTH - 1):
            conv = conv + cw[k:k + 1] * ext[k * nb:k * nb + rows]
        y = _dot((b_gate * conv).astype(BF16), wout_ref[...])
        o_ref[sub * ts:(sub + 1) * ts] = (h + y).reshape(ts, nb, d)
    carry_ref[...] = u_prev


def _conv_layer_to_time_major(h, g, w_in, cw, w_out, cast=()):
    bsz, seq, d = h.shape
    assert bsz == SUBLANES
    tt = TS_CONV // bsz
    cast_in, cast_out, cast_shapes = _cast_side_job(cast, seq // tt)
    out = pl.pallas_call(
        _conv_layer_tm_kernel,
        out_shape=[jax.ShapeDtypeStruct((seq, bsz, d), h.dtype)] + cast_shapes,
        grid=(seq // tt,),
        in_specs=[
            pl.BlockSpec((bsz, tt, d), lambda s: (0, s, 0)),
            _const_spec(g.shape), _const_spec(w_in.shape), _const_spec(cw.shape),
            _const_spec(w_out.shape),
        ] + cast_in,
        out_specs=[pl.BlockSpec((tt, bsz, d), lambda s: (s, 0, 0))] + cast_out,
        scratch_shapes=[pltpu.VMEM(((CONV_WIDTH - 1) * bsz, d), F32)],
        compiler_params=pltpu.CompilerParams(
            dimension_semantics=("arbitrary",), vmem_limit_bytes=VMEM_LIMIT),
        name="conv_layer_tm",
    )(h, g, w_in, cw, w_out, *[stacked for stacked, _ in cast])
    return out[0], tuple(out[1:])


def _lru_layer_kernel(h_ref, g_ref, win_ref, cw_ref, cb_ref, wax_ref, ba_ref, bx_ref, lam_ref,
                      wout_ref, o_ref, xcarry_ref, hcarry_ref):
    @pl.when(pl.program_id(0) == 0)
    def _():
        xcarry_ref[...] = jnp.zeros_like(xcarry_ref)
        hcarry_ref[...] = jnp.zeros_like(hcarry_ref)

    tt, nb, d = h_ref.shape
    ts = tt // LRU_SUBTILES
    rows = ts * nb
    halo = (LRU_CONV_WIDTH - 1) * nb
    cw = cw_ref[...]
    lam = lam_ref[...]
    log_sig_lam = jnp.minimum(lam, 0.0) - jnp.log1p(jnp.exp(-jnp.abs(lam)))
    neg_half_c_lsl = (-0.5 * LRU_C) * log_sig_lam
    ba = ba_ref[...]
    bx = bx_ref[...]
    x_prev = xcarry_ref[...]
    h_prev = hcarry_ref[...]

    def project_in(sub):
        h = h_ref[sub * ts:(sub + 1) * ts].reshape(rows, d)
        xn = _rms(h, g_ref[...]).astype(BF16)
        return h, _dot(xn, win_ref[...])

    projected = project_in(0)
    for sub in range(LRU_SUBTILES):
        t0 = sub * ts
        h, gr = projected
        if sub + 1 < LRU_SUBTILES:
            projected = project_in(sub + 1)
        gate = _gelu_tanh(gr[:, :LRU_WIDTH])
        xr = gr[:, LRU_WIDTH:]
        ext = jnp.concatenate([x_prev, xr], axis=0)
        x_prev = xr[rows - halo:]
        rec = cb_ref[...] + cw[LRU_CONV_WIDTH - 1:LRU_CONV_WIDTH] * xr
        for k in range(LRU_CONV_WIDTH - 1):
            rec = rec + cw[k:k + 1] * ext[k * nb:k * nb + rows]

        outs = []
        for n in range(LRU_BLOCKS):
            sl = slice(n * LRU_BLOCK_W, (n + 1) * LRU_BLOCK_W)
            xb = rec[:, sl]
            ri = _dot(xb.astype(BF16), wax_ref[n])
            neg_log_a = neg_half_c_lsl[:, sl] * (jnp.tanh(ri[:, :LRU_BLOCK_W] + ba[:, sl]) + 1.0)
            i = 0.5 * jnp.tanh(ri[:, LRU_BLOCK_W:] + bx[:, sl]) + 0.5
            a = jnp.exp2(neg_log_a * (-LOG2E))
            z = jnp.tanh(neg_log_a) * (a * a + 1.0)
            mult = jnp.where(z > 0.0, z * lax.rsqrt(z), 0.0)
            b = mult * (i * xb)
            hp = h_prev[:, sl]
            steps = []
            for t in range(ts):
                hp = a[t * nb:(t + 1) * nb] * hp + b[t * nb:(t + 1) * nb]
                steps.append(hp)
            outs.append(jnp.concatenate(steps, axis=0))
        hs = jnp.concatenate(outs, axis=1)
        h_prev = hs[rows - nb:]
        y = _dot((gate * hs).astype(BF16), wout_ref[...])
        o_ref[t0:t0 + ts] = (h + y).reshape(ts, nb, d)

    xcarry_ref[...] = x_prev
    hcarry_ref[...] = h_prev


def _lru_layer(h, g, w_in, cw, cb, wax, ba, bx, lam, w_out):
    seq, bsz, d = h.shape
    assert bsz == SUBLANES
    tt = TT_LRU
    consts = (g, w_in, cw, cb, wax, ba, bx, lam, w_out)
    tok = pl.BlockSpec((tt, bsz, d), lambda s: (s, 0, 0))
    return pl.pallas_call(
        _lru_layer_kernel,
        out_shape=jax.ShapeDtypeStruct(h.shape, h.dtype),
        grid=(seq // tt,),
        in_specs=[tok] + [_const_spec(c.shape) for c in consts],
        out_specs=tok,
        scratch_shapes=[pltpu.VMEM(((LRU_CONV_WIDTH - 1) * bsz, LRU_WIDTH), F32),
                        pltpu.VMEM((bsz, LRU_WIDTH), F32)],
        compiler_params=pltpu.CompilerParams(
            dimension_semantics=("arbitrary",), vmem_limit_bytes=VMEM_LIMIT),
        name="lru_layer",
    )(h, *consts)


def _rope_table_kernel(pos_ref, freq_ref, live_ref, sign_ref, cos_ref, sin_ref):
    for r in range(pos_ref.shape[0]):
        ang = freq_ref[...] * pos_ref[r:r + 1, :]
        reps = LANES // ang.shape[0]
        cos_t = jnp.concatenate([jnp.cos(ang)] * reps, axis=0).T
        sin_t = jnp.concatenate([jnp.sin(ang)] * reps, axis=0).T
        cos_ref[r * LANES:(r + 1) * LANES, :] = cos_t * live_ref[...]
        sin_ref[r * LANES:(r + 1) * LANES, :] = sin_t * sign_ref[...]


def _rope_tables(positions, inv_freq, live_slab, sign_slab):
    bsz, seq = positions.shape
    n = bsz * seq
    pos2d = positions.astype(F32).reshape(n // LANES, LANES)
    rows = ROPE_TABLE_POS_ROWS * LANES
    table = jax.ShapeDtypeStruct((n, LANES), F32)
    small = lambda shape: pl.BlockSpec(shape, lambda i: (0, 0))
    cos, sin = pl.pallas_call(
        _rope_table_kernel,
        out_shape=[table, table],
        grid=(n // rows,),
        in_specs=[pl.BlockSpec((ROPE_TABLE_POS_ROWS, LANES), lambda i: (i, 0)),
                  small((inv_freq.shape[0], 1)), small((1, LANES)), small((1, LANES))],
        out_specs=[pl.BlockSpec((rows, LANES), lambda i: (i, 0))] * 2,
        name="rope_tables",
    )(pos2d, inv_freq.reshape(-1, 1), live_slab, sign_slab)
    return cos.reshape(bsz, seq, LANES), sin.reshape(bsz, seq, LANES)


def _mla_proj_kernel(h_ref, cos_ref, sin_ref, g_ref, wd_ref, qg_ref, kvg_ref, wuq_ref, wukv_ref,
                     qng_ref, qrg_ref, kng_ref, krg_ref, pair_ref,
                     q_ref, k_ref, v_ref):
    ts = h_ref.shape[1] // MLA_SUBTILES
    nope_all = MLA_HEADS * QK_NOPE_DIM
    half = QK_ROPE_DIM // 2
    q_scale = LOG2E / math.sqrt(QK_NOPE_DIM + QK_ROPE_DIM)
    qn_gain = qng_ref[...] * q_scale
    qr_gain = qrg_ref[...] * q_scale
    ones = jnp.ones((ts, LANES), BF16)

    def slab_norm_scales(x):
        scales = []
        for p in range(x.shape[1] // (2 * LANES)):
            xs = x[:, p * 2 * LANES:(p + 1) * 2 * LANES]
            r = lax.rsqrt(_dot((xs * xs).astype(BF16), pair_ref[...]) + NORM_EPS)
            scales += [r[:, :LANES], r[:, LANES:]]
        return scales

    def project(sub):
        rows = pl.ds(sub * ts, ts)
        xn = _rms(h_ref[0, rows, :], g_ref[...]).astype(BF16)
        c = _dot(xn, wd_ref[...])
        c_q = c[:, :Q_LORA_RANK]
        c_kv = c[:, Q_LORA_RANK:Q_LORA_RANK + KV_LORA_RANK]
        k_rope = c[:, Q_LORA_RANK + KV_LORA_RANK:]
        q = _dot(_rms(c_q, qg_ref[...]).astype(BF16), wuq_ref[...])
        kv = _dot(_rms(c_kv, kvg_ref[...]).astype(BF16), wukv_ref[...])
        rq_nope = slab_norm_scales(q[:, :nope_all])
        rq_rope = slab_norm_scales(q[:, nope_all:])
        rk_nope = slab_norm_scales(kv[:, :nope_all])
        rk_rope = lax.rsqrt(_dot((k_rope * k_rope).astype(BF16), pair_ref[:LANES, :LANES]) + NORM_EPS)
        return q, kv, k_rope, rq_nope, rq_rope, rk_nope, rk_rope

    def finish(sub, projected):
        q, kv, k_rope, rq_nope, rq_rope, rk_nope, rk_rope = projected
        rows = pl.ds(sub * ts, ts)
        cos = cos_ref[0, rows, :]
        sin = sin_ref[0, rows, :]

        def rope(y):
            return y * cos + pltpu.roll(y, half, 1) * sin

        k_rope = rope(k_rope * rk_rope * krg_ref[...]).astype(BF16)
        for hh in range(MLA_HEADS):
            sl = slice(hh * LANES, (hh + 1) * LANES)
            sl2 = slice(nope_all + hh * LANES, nope_all + (hh + 1) * LANES)
            q_ref[0, hh, rows, :LANES] = (q[:, sl] * rq_nope[hh] * qn_gain).astype(BF16)
            q_ref[0, hh, rows, LANES:] = rope(q[:, sl2] * rq_rope[hh] * qr_gain).astype(BF16)
            k_ref[0, hh, rows, :LANES] = (kv[:, sl] * rk_nope[hh] * kng_ref[...]).astype(BF16)
            k_ref[0, hh, rows, LANES:] = k_rope
            v_ref[0, hh, rows, :LANES] = kv[:, sl2].astype(BF16)
            v_ref[0, hh, rows, LANES:] = ones

    projected = project(0)
    for sub in range(MLA_SUBTILES):
        current = projected
        if sub + 1 < MLA_SUBTILES:
            projected = project(sub + 1)
        finish(sub, current)


def _mla_proj(h, cos_tab, sin_tab, consts):
    bsz, seq, d = h.shape
    ts = TS_MLA
    out_sds = jax.ShapeDtypeStruct((bsz, MLA_HEADS, seq, QK_DIM_PADDED), BF16)
    out_spec = pl.BlockSpec((1, MLA_HEADS, ts, QK_DIM_PADDED), lambda b, s: (b, 0, s, 0))
    slab_spec = pl.BlockSpec((1, ts, LANES), lambda b, s: (b, s, 0))
    return pl.pallas_call(
        _mla_proj_kernel,
        out_shape=(out_sds, out_sds, out_sds),
        grid=(bsz, seq // ts),
        in_specs=[pl.BlockSpec((1, ts, d), lambda b, s: (b, s, 0)), slab_spec, slab_spec]
        + [_const_spec(c.shape) for c in consts],
        out_specs=(out_spec, out_spec, out_spec),
        compiler_params=pltpu.CompilerParams(
            dimension_semantics=("arbitrary", "arbitrary"), vmem_limit_bytes=VMEM_LIMIT),
        name="mla_proj",
    )(h, cos_tab, sin_tab, *consts)


def _attn_kernel(q_ref, k_ref, v_ref, o_ref, m_ref, acc_ref, *s_refs):
    qi = pl.program_id(2)
    nhp, tq = q_ref.shape[1], q_ref.shape[2]
    tk = tq
    n_groups = len(s_refs)
    ng = nhp // n_groups
    m_ref[...] = jnp.full_like(m_ref, -jnp.inf)
    acc_ref[...] = jnp.zeros_like(acc_ref)

    def logits(kb, group):
        start = pl.multiple_of(kb * tk, tk)
        for i in range(ng):
            hp = group * ng + i
            k = k_ref[0, hp, pl.ds(start, tk), :]
            s_refs[group][i] = lax.dot_general(q_ref[0, hp], k, (((1,), (1,)), ((), ())),
                                               preferred_element_type=F32)

    def softmax_pv(kb, group, masked):
        start = pl.multiple_of(kb * tk, tk)
        for i in range(ng):
            hp = group * ng + i
            s = s_refs[group][i]
            if masked:
                row = lax.broadcasted_iota(jnp.int32, s.shape, 0)
                col = lax.broadcasted_iota(jnp.int32, s.shape, 1)
                s = jnp.where(col <= row, s, jnp.finfo(F32).min)
            m_prev = m_ref[hp]
            m_next = jnp.maximum(m_prev, jnp.max(s, axis=1, keepdims=True))
            alpha = jnp.exp2(m_prev - m_next)
            p = jnp.exp2(s - jnp.concatenate([m_next] * (tk // LANES), axis=1))
            v = v_ref[0, hp, pl.ds(start, tk), :]
            acc_ref[hp] = (jnp.concatenate([alpha, alpha], axis=1) * acc_ref[hp]
                           + _dot(p.astype(BF16), v))
            m_ref[hp] = m_next

    logits(0, 0)

    def unmasked_block(kb):
        for group in range(n_groups):
            if group + 1 < n_groups:
                logits(kb, group + 1)
            else:
                logits(kb + 1, 0)
            softmax_pv(kb, group, False)

    odd = qi % 2

    @pl.when(odd == 1)
    def _():
        unmasked_block(0)

    def body(j, carry):
        unmasked_block(odd + 2 * j)
        unmasked_block(odd + 2 * j + 1)
        return carry

    lax.fori_loop(0, qi // 2, body, 0)
    for group in range(n_groups):
        if group + 1 < n_groups:
            logits(qi, group + 1)
        softmax_pv(qi, group, True)

    for hp in range(nhp):
        acc = acc_ref[hp]
        o_ref[0, :, hp * LANES:(hp + 1) * LANES] = (acc[:, :LANES] / acc[:, LANES:]).astype(o_ref.dtype)


def _attention(q, k, v1):
    bsz, nh, seq, dq = q.shape
    tq = TQ_ATTN
    hp = ATTN_HEADS_PER_STEP
    return pl.pallas_call(
        _attn_kernel,
        out_shape=jax.ShapeDtypeStruct((bsz, seq, nh * V_HEAD_DIM), BF16),
        grid=(bsz, nh // hp, seq // tq),
        in_specs=[pl.BlockSpec((1, hp, tq, dq), lambda b, h, i: (b, h, i, 0)),
                  pl.BlockSpec((1, hp, seq, dq), lambda b, h, i: (b, h, 0, 0)),
                  pl.BlockSpec((1, hp, seq, dq), lambda b, h, i: (b, h, 0, 0))],
        out_specs=pl.BlockSpec((1, tq, hp * V_HEAD_DIM), lambda b, h, i: (b, i, h)),
        scratch_shapes=[pltpu.VMEM((hp, tq, LANES), F32), pltpu.VMEM((hp, tq, dq), F32)]
        + [pltpu.VMEM((hp // ATTN_GROUPS, tq, tq), F32)] * ATTN_GROUPS,
        compiler_params=pltpu.CompilerParams(
            dimension_semantics=("arbitrary", "arbitrary", "arbitrary"),
            vmem_limit_bytes=VMEM_LIMIT),
        name="mla_attention",
    )(q, k, v1)


def _ffn_subtiles(load, store, g_ref, wgu_ref, wdn_ref):
    def prepare(i):
        h = load(i)
        return h, _rms(h, g_ref[...]).astype(BF16)

    prepared = prepare(0)
    for i in range(FFN_SUBTILES):
        h, xn = prepared
        out = h
        for ci, (lo, hi) in enumerate(FFN_CHUNK_BOUNDS):
            gate = _dot(xn, wgu_ref[:, lo:hi])
            up = _dot(xn, wgu_ref[:, D_FF + lo:D_FF + hi])
            if ci == 0 and i + 1 < FFN_SUBTILES:
                prepared = prepare(i + 1)
            act = (jax.nn.silu(gate) * up).astype(BF16)
            out = out + _dot(act, wdn_ref[lo:hi, :])
        store(i, out)


def _ffn_kernel(h_ref, g_ref, wgu_ref, wdn_ref, *rest):
    cast_srcs, o_ref, cast_dsts = _split_cast_refs(rest)
    _run_casts(cast_srcs, cast_dsts)
    rows = h_ref.shape[0] // FFN_SUBTILES

    def store(i, out):
        o_ref[i * rows:(i + 1) * rows] = out

    _ffn_subtiles(lambda i: h_ref[i * rows:(i + 1) * rows], store, g_ref, wgu_ref, wdn_ref)


def _ffn_from_time_major_kernel(h_ref, g_ref, wgu_ref, wdn_ref, *rest):
    cast_srcs, o_ref, cast_dsts = _split_cast_refs(rest)
    _run_casts(cast_srcs, cast_dsts)
    tt, nb, d = h_ref.shape
    ts = tt // FFN_SUBTILES

    def store(i, out):
        o_ref[:, i * ts:(i + 1) * ts, :] = jnp.swapaxes(out.reshape(ts, nb, d), 0, 1)

    _ffn_subtiles(lambda i: h_ref[i * ts:(i + 1) * ts].reshape(ts * nb, d), store,
                  g_ref, wgu_ref, wdn_ref)


def _proj_ffn_kernel(h_ref, a_ref, wo_ref, g_ref, wgu_ref, wdn_ref, *rest):
    cast_srcs, o_ref, cast_dsts = _split_cast_refs(rest)
    _run_casts(cast_srcs, cast_dsts)
    rows = h_ref.shape[0] // FFN_SUBTILES

    def load(i):
        sl = slice(i * rows, (i + 1) * rows)
        return h_ref[sl] + _dot(a_ref[sl], wo_ref[...])

    def store(i, out):
        o_ref[i * rows:(i + 1) * rows] = out

    _ffn_subtiles(load, store, g_ref, wgu_ref, wdn_ref)


def _ffn_call(kernel, name, n_steps, tokens, token_specs, consts, out_sds, out_spec, cast):
    cast_in, cast_out, cast_shapes = _cast_side_job(cast, n_steps)
    out = pl.pallas_call(
        kernel,
        out_shape=[out_sds] + cast_shapes,
        grid=(n_steps,),
        in_specs=list(token_specs) + [_const_spec(c.shape) for c in consts] + cast_in,
        out_specs=[out_spec] + cast_out,
        compiler_params=pltpu.CompilerParams(
            dimension_semantics=("arbitrary",), vmem_limit_bytes=VMEM_LIMIT),
        name=name,
    )(*tokens, *consts, *[stacked for stacked, _ in cast])
    return out[0], tuple(out[1:])


def _ffn_from_time_major(h, g, w_gu, w_dn, cast=()):
    seq, bsz, d = h.shape
    tt = TM_FFN // bsz
    return _ffn_call(_ffn_from_time_major_kernel, "ffn_tm", seq // tt, (h,),
                     [pl.BlockSpec((tt, bsz, d), lambda s: (s, 0, 0))], (g, w_gu, w_dn),
                     jax.ShapeDtypeStruct((bsz, seq, d), h.dtype),
                     pl.BlockSpec((bsz, tt, d), lambda s: (0, s, 0)), cast)


def _ffn(h2, g, w_gu, w_dn, cast=()):
    t, d = h2.shape
    tok = pl.BlockSpec((TM_FFN, d), lambda i: (i, 0))
    return _ffn_call(_ffn_kernel, "ffn", t // TM_FFN, (h2,), [tok], (g, w_gu, w_dn),
                     jax.ShapeDtypeStruct(h2.shape, h2.dtype), tok, cast)


def _proj_ffn(h2, attn, w_o, g, w_gu, w_dn, cast=()):
    t, d = h2.shape
    tok = pl.BlockSpec((TM_FFN, d), lambda i: (i, 0))
    return _ffn_call(_proj_ffn_kernel, "proj_ffn", t // TM_FFN, (h2, attn),
                     [tok, pl.BlockSpec((TM_FFN, attn.shape[1]), lambda i: (i, 0))],
                     (w_o, g, w_gu, w_dn), jax.ShapeDtypeStruct((t, d), h2.dtype), tok, cast)


def _row(v):
    return v.reshape(1, -1)


def _pad_lanes(v, n):
    return jnp.pad(v, [(0, 0)] * (v.ndim - 1) + [(0, n - v.shape[-1])])


def _dup(v):
    return jnp.concatenate([v, v], axis=-1)


def kernel(x, positions, mix_norm, conv_w_in, conv_w, conv_w_out, lru_w_in, lru_conv_w, lru_conv_b, lru_gate_a_w, lru_gate_a_b, lru_gate_x_w, lru_gate_x_b, lru_lambda, lru_w_out, mla_w_down, mla_q_norm, mla_kv_norm, mla_w_uq, mla_w_ukv, mla_qn_norm, mla_qr_norm, mla_kn_norm, mla_kr_norm, mla_w_o, ffn_norm, ffn_w_gu, ffn_w_down):
    bsz, seq, d = x.shape
    nh = MLA_HEADS
    half = QK_ROPE_DIM // 2

    inv_freq = ROPE_THETA ** (-jnp.arange(0, QK_ROPE_DIM, 2, dtype=F32) / QK_ROPE_DIM)
    sign_slab = _row(_pad_lanes(jnp.concatenate([-jnp.ones(half, F32), jnp.ones(half, F32)]), LANES))
    live_slab = _row(_pad_lanes(jnp.ones(QK_ROPE_DIM, F32), LANES))
    cos_tab, sin_tab = _rope_tables(positions, inv_freq, live_slab, sign_slab)
    lane_slab = jnp.arange(2 * LANES) // LANES
    pair_ones = ((lane_slab[:, None] == lane_slab[None, :]) * (1.0 / LANES)).astype(BF16)

    def mixer_jobs(i):
        kind, j = i % N_MIXERS, i // N_MIXERS
        if kind == 0:
            return ((conv_w_in, j), (conv_w_out, j))
        if kind == 1:
            return ((lru_w_in, j), (lru_w_out, j))
        return ((mla_w_o, j),)

    def ffn_jobs(i):
        return ((ffn_w_gu, i), (ffn_w_down, i))

    def cast_now(jobs):
        return tuple(stacked[index].astype(BF16) for stacked, index in jobs)

    h = x
    mixer_w = cast_now(mixer_jobs(0))
    ffn_w = None
    for i in range(DEPTH):
        kind, j = i % N_MIXERS, i // N_MIXERS
        g = _row(mix_norm[i])
        gf = _row(ffn_norm[i])
        last = i + 1 == DEPTH
        next_jobs = () if last else mixer_jobs(i + 1) + ffn_jobs(i + 1)
        to_time_major = kind == 0 and not last and (i + 1) % N_MIXERS == 1
        if ffn_w is None and not to_time_major:
            ffn_w = cast_now(ffn_jobs(i))
        if kind == 0:
            conv_args = (h, g, mixer_w[0], conv_w[j], mixer_w[1])
            if to_time_major:
                h, cast_ffn = _conv_layer_to_time_major(
                    *conv_args, cast=() if ffn_w else ffn_jobs(i))
                h, casts = _ffn(h.reshape(seq * bsz, d), gf, *(ffn_w or cast_ffn), cast=next_jobs)
                h = h.reshape(seq, bsz, d)
            else:
                h = _conv_layer(*conv_args)
                h, casts = _ffn(h.reshape(bsz * seq, d), gf, *ffn_w, cast=next_jobs)
                h = h.reshape(bsz, seq, d)
        elif kind == 1:
            wax = (0.5 * jnp.concatenate([lru_gate_a_w[j], lru_gate_x_w[j]], axis=-1)).astype(BF16)
            h = _lru_layer(h, g, mixer_w[0], lru_conv_w[j], _row(lru_conv_b[j]),
                           wax, _row(0.5 * lru_gate_a_b[j]), _row(0.5 * lru_gate_x_b[j]),
                           _row(lru_lambda[j]), mixer_w[1])
            h, casts = _ffn_from_time_major(h, gf, *ffn_w, cast=next_jobs)
        else:
            wdm = mla_w_down[j]
            wd = jnp.concatenate([wdm, wdm[:, Q_LORA_RANK + KV_LORA_RANK:]], axis=1).astype(BF16)
            wq = mla_w_uq[j].reshape(Q_LORA_RANK, nh, QK_NOPE_DIM + QK_ROPE_DIM)
            wuq = jnp.concatenate(
                [wq[:, :, :QK_NOPE_DIM].reshape(Q_LORA_RANK, nh * QK_NOPE_DIM),
                 _dup(wq[:, :, QK_NOPE_DIM:]).reshape(Q_LORA_RANK, nh * LANES)],
                axis=1).astype(BF16)
            wkv = mla_w_ukv[j].reshape(KV_LORA_RANK, nh, QK_NOPE_DIM + V_HEAD_DIM)
            wukv = jnp.concatenate(
                [wkv[:, :, :QK_NOPE_DIM].reshape(KV_LORA_RANK, nh * QK_NOPE_DIM),
                 wkv[:, :, QK_NOPE_DIM:].reshape(KV_LORA_RANK, nh * V_HEAD_DIM)], axis=1).astype(BF16)
            consts = (g, wd, _row(mla_q_norm[j]), _row(mla_kv_norm[j]), wuq, wukv,
                      _row(mla_qn_norm[j]), _row(_dup(mla_qr_norm[j])),
                      _row(mla_kn_norm[j]), _row(_dup(mla_kr_norm[j])),
                      pair_ones)
            q, k, v1 = _mla_proj(h, cos_tab, sin_tab, consts)
            attn = _attention(q, k, v1).reshape(bsz * seq, nh * V_HEAD_DIM)
            h, casts = _proj_ffn(h.reshape(bsz * seq, d), attn, mixer_w[0], gf, *ffn_w,
                                 cast=next_jobs)
            h = h.reshape(bsz, seq, d)
        if not last:
            n_mixer = len(mixer_jobs(i + 1))
            mixer_w, ffn_w = casts[:n_mixer], casts[n_mixer:]
    return h
```

```python
import math

import jax
import jax.numpy as jnp
from jax import lax
from jax.experimental import pallas as pl
from jax.experimental.pallas import tpu as pltpu

F32 = jnp.float32
BF16 = jnp.bfloat16

D_MODEL = 1024
DEPTH = 4
N_MIXERS = 3
CONV_WIDTH = 3
LRU_WIDTH = 1280
LRU_BLOCKS = 10
LRU_BLOCK_W = LRU_WIDTH // LRU_BLOCKS
LRU_CONV_WIDTH = 4
LRU_C = 8.0
MLA_HEADS = 8
Q_LORA_RANK = 384
KV_LORA_RANK = 256
QK_NOPE_DIM = 128
QK_ROPE_DIM = 64
V_HEAD_DIM = 128
ROPE_THETA = 10000.0
D_FF = 2816
NORM_EPS = 1e-6

LANES = 128
SUBLANES = 8
BF16_SUBLANES = 16
QK_DIM_PADDED = 2 * LANES
VMEM_LIMIT = 56 * 1024 * 1024

TS_CONV = 1024
CONV_SUBTILES = 2
TT_LRU = 128
LRU_SUBTILES = 2
TS_MLA = 1024
ROPE_TABLE_POS_ROWS = 32
MLA_SUBTILES = 4
TQ_ATTN = 512
ATTN_HEADS_PER_STEP = 4
ATTN_GROUPS = 4
TM_FFN = 1024
FFN_SUBTILES = 4
MXU_DIM = 256
_FFN_SPLIT = (D_FF // MXU_DIM + 1) // 2 * MXU_DIM
FFN_CHUNK_BOUNDS = ((0, _FFN_SPLIT), (_FFN_SPLIT, D_FF))
LOG2E = 1.4426950408889634


def _rms(x, g):
    return x * lax.rsqrt(jnp.mean(x * x, axis=-1, keepdims=True) + NORM_EPS) * g


def _dot(a, b):
    return jnp.dot(a, b, preferred_element_type=F32)


def _gelu_tanh(x):
    k0 = -2.0 * LOG2E * math.sqrt(2.0 / math.pi)
    return x / (1.0 + jnp.exp2(x * (k0 + (k0 * 0.044715) * (x * x))))


def _cast_side_job(jobs, n_steps):
    in_specs, out_specs, out_shapes = [], [], []
    for stacked, index in jobs:
        _, rows, cols = stacked.shape
        n_blocks = n_steps
        while rows % n_blocks or (rows // n_blocks) % BF16_SUBLANES:
            n_blocks //= 2
        rep = n_steps // n_blocks
        in_specs.append(pl.BlockSpec((None, rows // n_blocks, cols),
                                     lambda i, rep=rep, index=index: (index, i // rep, 0)))
        out_specs.append(pl.BlockSpec((rows // n_blocks, cols), lambda i, rep=rep: (i // rep, 0)))
        out_shapes.append(jax.ShapeDtypeStruct((rows, cols), BF16))
    return in_specs, out_specs, out_shapes


def _split_cast_refs(refs):
    n = (len(refs) - 1) // 2
    return refs[:n], refs[n], refs[n + 1:]


def _run_casts(srcs, dsts):
    for src, dst in zip(srcs, dsts):
        dst[...] = src[...].astype(dst.dtype)


def _const_spec(shape):
    nd = len(shape)
    return pl.BlockSpec(shape, lambda *_: (0,) * nd, pipeline_mode=pl.Buffered(1))


def _shift_rows_prev(x, prev8, d):
    r = pltpu.roll(x, d, 0)
    hd = pltpu.roll(prev8, d, 0)
    rows8 = lax.broadcasted_iota(jnp.int32, prev8.shape, 0)
    first = jnp.where(rows8 < d, hd, r[:SUBLANES])
    return jnp.concatenate([first, r[SUBLANES:]], axis=0)


def _conv_layer_kernel(h_ref, g_ref, win_ref, cw_ref, wout_ref, o_ref, carry_ref):
    @pl.when(pl.program_id(1) == 0)
    def _():
        carry_ref[...] = jnp.zeros_like(carry_ref)

    rows = h_ref.shape[1] // CONV_SUBTILES
    cw = cw_ref[...]

    def project_in(sub):
        h = h_ref[0, sub * rows:(sub + 1) * rows, :]
        return h, _dot(_rms(h, g_ref[...]).astype(BF16), win_ref[...])

    prev8 = carry_ref[...]
    projected = project_in(0)
    for sub in range(CONV_SUBTILES):
        h, bch = projected
        if sub + 1 < CONV_SUBTILES:
            projected = project_in(sub + 1)
        b_gate = bch[:, :D_MODEL]
        u = bch[:, D_MODEL:2 * D_MODEL] * bch[:, 2 * D_MODEL:]
        conv = (cw[0:1] * _shift_rows_prev(u, prev8, 2)
                + cw[1:2] * _shift_rows_prev(u, prev8, 1)
                + cw[2:3] * u)
        prev8 = u[rows - SUBLANES:]
        y = _dot((b_gate * conv).astype(BF16), wout_ref[...])
        o_ref[0, sub * rows:(sub + 1) * rows, :] = h + y
    carry_ref[...] = prev8


def _conv_layer(h, g, w_in, cw, w_out):
    bsz, seq, d = h.shape
    ts = TS_CONV
    tok = pl.BlockSpec((1, ts, d), lambda b, s: (b, s, 0))
    return pl.pallas_call(
        _conv_layer_kernel,
        out_shape=jax.ShapeDtypeStruct(h.shape, h.dtype),
        grid=(bsz, seq // ts),
        in_specs=[
            tok, _const_spec(g.shape), _const_spec(w_in.shape), _const_spec(cw.shape),
            _const_spec(w_out.shape),
        ],
        out_specs=tok,
        scratch_shapes=[pltpu.VMEM((SUBLANES, d), F32)],
        compiler_params=pltpu.CompilerParams(
            dimension_semantics=("arbitrary", "arbitrary"), vmem_limit_bytes=VMEM_LIMIT),
        name="conv_layer",
    )(h, g, w_in, cw, w_out)


def _conv_layer_tm_kernel(h_ref, g_ref, win_ref, cw_ref, wout_ref, *rest):
    cast_srcs, o_ref, cast_dsts = _split_cast_refs(rest[:-1])
    carry_ref = rest[-1]
    _run_casts(cast_srcs, cast_dsts)

    @pl.when(pl.program_id(0) == 0)
    def _():
        carry_ref[...] = jnp.zeros_like(carry_ref)

    nb, tt, d = h_ref.shape
    ts = tt // CONV_SUBTILES
    rows = ts * nb
    halo = (CONV_WIDTH - 1) * nb
    cw = cw_ref[...]

    def project_in(sub):
        h = jnp.swapaxes(h_ref[:, sub * ts:(sub + 1) * ts, :], 0, 1).reshape(rows, d)
        return h, _dot(_rms(h, g_ref[...]).astype(BF16), win_ref[...])

    u_prev = carry_ref[...]
    projected = project_in(0)
    for sub in range(CONV_SUBTILES):
        h, bch = projected
        if sub + 1 < CONV_SUBTILES:
            projected = project_in(sub + 1)
        b_gate = bch[:, :D_MODEL]
        u = bch[:, D_MODEL:2 * D_MODEL] * bch[:, 2 * D_MODEL:]
        ext = jnp.concatenate([u_prev, u], axis=0)
        u_prev = u[rows - halo:]
        conv = cw[CONV_WIDTH - 1:CONV_WIDTH] * u
        for k in range(CONV_WIDTH - 1):
            conv = conv + cw[k:k + 1] * ext[k * nb:k * nb + rows]
        y = _dot((b_gate * conv).astype(BF16), wout_ref[...])
        o_ref[sub * ts:(sub + 1) * ts] = (h + y).reshape(ts, nb, d)
    carry_ref[...] = u_prev


def _conv_layer_to_time_major(h, g, w_in, cw, w_out, cast=()):
    bsz, seq, d = h.shape
    assert bsz == SUBLANES
    tt = TS_CONV // bsz
    cast_in, cast_out, cast_shapes = _cast_side_job(cast, seq // tt)
    out = pl.pallas_call(
        _conv_layer_tm_kernel,
        out_shape=[jax.ShapeDtypeStruct((seq, bsz, d), h.dtype)] + cast_shapes,
        grid=(seq // tt,),
        in_specs=[
            pl.BlockSpec((bsz, tt, d), lambda s: (0, s, 0)),
            _const_spec(g.shape), _const_spec(w_in.shape), _const_spec(cw.shape),
            _const_spec(w_out.shape),
        ] + cast_in,
        out_specs=[pl.BlockSpec((tt, bsz, d), lambda s: (s, 0, 0))] + cast_out,
        scratch_shapes=[pltpu.VMEM(((CONV_WIDTH - 1) * bsz, d), F32)],
        compiler_params=pltpu.CompilerParams(
            dimension_semantics=("arbitrary",), vmem_limit_bytes=VMEM_LIMIT),
        name="conv_layer_tm",
    )(h, g, w_in, cw, w_out, *[stacked for stacked, _ in cast])
    return out[0], tuple(out[1:])


def _lru_layer_kernel(h_ref, g_ref, win_ref, cw_ref, cb_ref, wax_ref, ba_ref, bx_ref, lam_ref,
                      wout_ref, o_ref, xcarry_ref, hcarry_ref):
    @pl.when(pl.program_id(0) == 0)
    def _():
        xcarry_ref[...] = jnp.zeros_like(xcarry_ref)
        hcarry_ref[...] = jnp.zeros_like(hcarry_ref)

    tt, nb, d = h_ref.shape
    ts = tt // LRU_SUBTILES
    rows = ts * nb
    halo = (LRU_CONV_WIDTH - 1) * nb
    cw = cw_ref[...]
    lam = lam_ref[...]
    log_sig_lam = jnp.minimum(lam, 0.0) - jnp.log1p(jnp.exp(-jnp.abs(lam)))
    neg_half_c_lsl = (-0.5 * LRU_C) * log_sig_lam
    ba = ba_ref[...]
    bx = bx_ref[...]
    x_prev = xcarry_ref[...]
    h_prev = hcarry_ref[...]

    def project_in(sub):
        h = h_ref[sub * ts:(sub + 1) * ts].reshape(rows, d)
        xn = _rms(h, g_ref[...]).astype(BF16)
        return h, _dot(xn, win_ref[...])

    projected = project_in(0)
    for sub in range(LRU_SUBTILES):
        t0 = sub * ts
        h, gr = projected
        if sub + 1 < LRU_SUBTILES:
            projected = project_in(sub + 1)
        gate = _gelu_tanh(gr[:, :LRU_WIDTH])
        xr = gr[:, LRU_WIDTH:]
        ext = jnp.concatenate([x_prev, xr], axis=0)
        x_prev = xr[rows - halo:]
        rec = cb_ref[...] + cw[LRU_CONV_WIDTH - 1:LRU_CONV_WIDTH] * xr
        for k in range(LRU_CONV_WIDTH - 1):
            rec = rec + cw[k:k + 1] * ext[k * nb:k * nb + rows]

        outs = []
        for n in range(LRU_BLOCKS):
            sl = slice(n * LRU_BLOCK_W, (n + 1) * LRU_BLOCK_W)
            xb = rec[:, sl]
            ri = _dot(xb.astype(BF16), wax_ref[n])
            neg_log_a = neg_half_c_lsl[:, sl] * (jnp.tanh(ri[:, :LRU_BLOCK_W] + ba[:, sl]) + 1.0)
            i = 0.5 * jnp.tanh(ri[:, LRU_BLOCK_W:] + bx[:, sl]) + 0.5
            a = jnp.exp2(neg_log_a * (-LOG2E))
            z = jnp.tanh(neg_log_a) * (a * a + 1.0)
            mult = jnp.where(z > 0.0, z * lax.rsqrt(z), 0.0)
            b = mult * (i * xb)
            hp = h_prev[:, sl]
            steps = []
            for t in range(ts):
                hp = a[t * nb:(t + 1) * nb] * hp + b[t * nb:(t + 1) * nb]
                steps.append(hp)
            outs.append(jnp.concatenate(steps, axis=0))
        hs = jnp.concatenate(outs, axis=1)
        h_prev = hs[rows - nb:]
        y = _dot((gate * hs).astype(BF16), wout_ref[...])
        o_ref[t0:t0 + ts] = (h + y).reshape(ts, nb, d)

    xcarry_ref[...] = x_prev
    hcarry_ref[...] = h_prev


def _lru_layer(h, g, w_in, cw, cb, wax, ba, bx, lam, w_out):
    seq, bsz, d = h.shape
    assert bsz == SUBLANES
    tt = TT_LRU
    consts = (g, w_in, cw, cb, wax, ba, bx, lam, w_out)
    tok = pl.BlockSpec((tt, bsz, d), lambda s: (s, 0, 0))
    return pl.pallas_call(
        _lru_layer_kernel,
        out_shape=jax.ShapeDtypeStruct(h.shape, h.dtype),
        grid=(seq // tt,),
        in_specs=[tok] + [_const_spec(c.shape) for c in consts],
        out_specs=tok,
        scratch_shapes=[pltpu.VMEM(((LRU_CONV_WIDTH - 1) * bsz, LRU_WIDTH), F32),
                        pltpu.VMEM((bsz, LRU_WIDTH), F32)],
        compiler_params=pltpu.CompilerParams(
            dimension_semantics=("arbitrary",), vmem_limit_bytes=VMEM_LIMIT),
        name="lru_layer",
    )(h, *consts)


def _rope_table_kernel(pos_ref, freq_ref, live_ref, sign_ref, cos_ref, sin_ref):
    for r in range(pos_ref.shape[0]):
        ang = freq_ref[...] * pos_ref[r:r + 1, :]
        reps = LANES // ang.shape[0]
        cos_t = jnp.concatenate([jnp.cos(ang)] * reps, axis=0).T
        sin_t = jnp.concatenate([jnp.sin(ang)] * reps, axis=0).T
        cos_ref[r * LANES:(r + 1) * LANES, :] = cos_t * live_ref[...]
        sin_ref[r * LANES:(r + 1) * LANES, :] = sin_t * sign_ref[...]


def _rope_tables(positions, inv_freq, live_slab, sign_slab):
    bsz, seq = positions.shape
    n = bsz * seq
    pos2d = positions.astype(F32).reshape(n // LANES, LANES)
    rows = ROPE_TABLE_POS_ROWS * LANES
    table = jax.ShapeDtypeStruct((n, LANES), F32)
    small = lambda shape: pl.BlockSpec(shape, lambda i: (0, 0))
    cos, sin = pl.pallas_call(
        _rope_table_kernel,
        out_shape=[table, table],
        grid=(n // rows,),
        in_specs=[pl.BlockSpec((ROPE_TABLE_POS_ROWS, LANES), lambda i: (i, 0)),
                  small((inv_freq.shape[0], 1)), small((1, LANES)), small((1, LANES))],
        out_specs=[pl.BlockSpec((rows, LANES), lambda i: (i, 0))] * 2,
        name="rope_tables",
    )(pos2d, inv_freq.reshape(-1, 1), live_slab, sign_slab)
    return cos.reshape(bsz, seq, LANES), sin.reshape(bsz, seq, LANES)


def _mla_proj_kernel(h_ref, cos_ref, sin_ref, g_ref, wd_ref, qg_ref, kvg_ref, wuq_ref, wukv_ref,
                     qng_ref, qrg_ref, kng_ref, krg_ref, pair_ref,
                     q_ref, k_ref, v_ref):
    ts = h_ref.shape[1] // MLA_SUBTILES
    nope_all = MLA_HEADS * QK_NOPE_DIM
    half = QK_ROPE_DIM // 2
    q_scale = LOG2E / math.sqrt(QK_NOPE_DIM + QK_ROPE_DIM)
    qn_gain = qng_ref[...] * q_scale
    qr_gain = qrg_ref[...] * q_scale
    ones = jnp.ones((ts, LANES), BF16)

    def slab_norm_scales(x):
        scales = []
        for p in range(x.shape[1] // (2 * LANES)):
            xs = x[:, p * 2 * LANES:(p + 1) * 2 * LANES]
            r = lax.rsqrt(_dot((xs * xs).astype(BF16), pair_ref[...]) + NORM_EPS)
            scales += [r[:, :LANES], r[:, LANES:]]
        return scales

    def project(sub):
        rows = pl.ds(sub * ts, ts)
        xn = _rms(h_ref[0, rows, :], g_ref[...]).astype(BF16)
        c = _dot(xn, wd_ref[...])
        c_q = c[:, :Q_LORA_RANK]
        c_kv = c[:, Q_LORA_RANK:Q_LORA_RANK + KV_LORA_RANK]
        k_rope = c[:, Q_LORA_RANK + KV_LORA_RANK:]
        q = _dot(_rms(c_q, qg_ref[...]).astype(BF16), wuq_ref[...])
        kv = _dot(_rms(c_kv, kvg_ref[...]).astype(BF16), wukv_ref[...])
        rq_nope = slab_norm_scales(q[:, :nope_all])
        rq_rope = slab_norm_scales(q[:, nope_all:])
        rk_nope = slab_norm_scales(kv[:, :nope_all])
        rk_rope = lax.rsqrt(_dot((k_rope * k_rope).astype(BF16), pair_ref[:LANES, :LANES]) + NORM_EPS)
        return q, kv, k_rope, rq_nope, rq_rope, rk_nope, rk_rope

    def finish(sub, projected):
        q, kv, k_rope, rq_nope, rq_rope, rk_nope, rk_rope = projected
        rows = pl.ds(sub * ts, ts)
        cos = cos_ref[0, rows, :]
        sin = sin_ref[0, rows, :]

        def rope(y):
            return y * cos + pltpu.roll(y, half, 1) * sin

        k_rope = rope(k_rope * rk_rope * krg_ref[...]).astype(BF16)
        for hh in range(MLA_HEADS):
            sl = slice(hh * LANES, (hh + 1) * LANES)
            sl2 = slice(nope_all + hh * LANES, nope_all + (hh + 1) * LANES)
            q_ref[0, hh, rows, :LANES] = (q[:, sl] * rq_nope[hh] * qn_gain).astype(BF16)
            q_ref[0, hh, rows, LANES:] = rope(q[:, sl2] * rq_rope[hh] * qr_gain).astype(BF16)
            k_ref[0, hh, rows, :LANES] = (kv[:, sl] * rk_nope[hh] * kng_ref[...]).astype(BF16)
            k_ref[0, hh, rows, LANES:] = k_rope
            v_ref[0, hh, rows, :LANES] = kv[:, sl2].astype(BF16)
            v_ref[0, hh, rows, LANES:] = ones

    projected = project(0)
    for sub in range(MLA_SUBTILES):
        current = projected
        if sub + 1 < MLA_SUBTILES:
            projected = project(sub + 1)
        finish(sub, current)


def _mla_proj(h, cos_tab, sin_tab, consts):
    bsz, seq, d = h.shape
    ts = TS_MLA
    out_sds = jax.ShapeDtypeStruct((bsz, MLA_HEADS, seq, QK_DIM_PADDED), BF16)
    out_spec = pl.BlockSpec((1, MLA_HEADS, ts, QK_DIM_PADDED), lambda b, s: (b, 0, s, 0))
    slab_spec = pl.BlockSpec((1, ts, LANES), lambda b, s: (b, s, 0))
    return pl.pallas_call(
        _mla_proj_kernel,
        out_shape=(out_sds, out_sds, out_sds),
        grid=(bsz, seq // ts),
        in_specs=[pl.BlockSpec((1, ts, d), lambda b, s: (b, s, 0)), slab_spec, slab_spec]
        + [_const_spec(c.shape) for c in consts],
        out_specs=(out_spec, out_spec, out_spec),
        compiler_params=pltpu.CompilerParams(
            dimension_semantics=("arbitrary", "arbitrary"), vmem_limit_bytes=VMEM_LIMIT),
        name="mla_proj",
    )(h, cos_tab, sin_tab, *consts)


def _attn_kernel(q_ref, k_ref, v_ref, o_ref, m_ref, acc_ref, *s_refs):
    qi = pl.program_id(2)
    nhp, tq = q_ref.shape[1], q_ref.shape[2]
    tk = tq
    n_groups = len(s_refs)
    ng = nhp // n_groups
    m_ref[...] = jnp.full_like(m_ref, -jnp.inf)
    acc_ref[...] = jnp.zeros_like(acc_ref)

    def logits(kb, group):
        start = pl.multiple_of(kb * tk, tk)
        for i in range(ng):
            hp = group * ng + i
            k = k_ref[0, hp, pl.ds(start, tk), :]
            s_refs[group][i] = lax.dot_general(q_ref[0, hp], k, (((1,), (1,)), ((), ())),
                                               preferred_element_type=F32)

    def softmax_pv(kb, group, masked):
        start = pl.multiple_of(kb * tk, tk)
        for i in range(ng):
            hp = group * ng + i
            s = s_refs[group][i]
            if masked:
                row = lax.broadcasted_iota(jnp.int32, s.shape, 0)
                col = lax.broadcasted_iota(jnp.int32, s.shape, 1)
                s = jnp.where(col <= row, s, jnp.finfo(F32).min)
            m_prev = m_ref[hp]
            m_next = jnp.maximum(m_prev, jnp.max(s, axis=1, keepdims=True))
            alpha = jnp.exp2(m_prev - m_next)
            p = jnp.exp2(s - jnp.concatenate([m_next] * (tk // LANES), axis=1))
            v = v_ref[0, hp, pl.ds(start, tk), :]
            acc_ref[hp] = (jnp.concatenate([alpha, alpha], axis=1) * acc_ref[hp]
                           + _dot(p.astype(BF16), v))
            m_ref[hp] = m_next

    logits(0, 0)

    def unmasked_block(kb):
        for group in range(n_groups):
            if group + 1 < n_groups:
                logits(kb, group + 1)
            else:
                logits(kb + 1, 0)
            softmax_pv(kb, group, False)

    odd = qi % 2

    @pl.when(odd == 1)
    def _():
        unmasked_block(0)

    def body(j, carry):
        unmasked_block(odd + 2 * j)
        unmasked_block(odd + 2 * j + 1)
        return carry

    lax.fori_loop(0, qi // 2, body, 0)
    for group in range(n_groups):
        if group + 1 < n_groups:
            logits(qi, group + 1)
        softmax_pv(qi, group, True)

    for hp in range(nhp):
        acc = acc_ref[hp]
        o_ref[0, :, hp * LANES:(hp + 1) * LANES] = (acc[:, :LANES] / acc[:, LANES:]).astype(o_ref.dtype)


def _attention(q, k, v1):
    bsz, nh, seq, dq = q.shape
    tq = TQ_ATTN
    hp = ATTN_HEADS_PER_STEP
    return pl.pallas_call(
        _attn_kernel,
        out_shape=jax.ShapeDtypeStruct((bsz, seq, nh * V_HEAD_DIM), BF16),
        grid=(bsz, nh // hp, seq // tq),
        in_specs=[pl.BlockSpec((1, hp, tq, dq), lambda b, h, i: (b, h, i, 0)),
                  pl.BlockSpec((1, hp, seq, dq), lambda b, h, i: (b, h, 0, 0)),
                  pl.BlockSpec((1, hp, seq, dq), lambda b, h, i: (b, h, 0, 0))],
        out_specs=pl.BlockSpec((1, tq, hp * V_HEAD_DIM), lambda b, h, i: (b, i, h)),
        scratch_shapes=[pltpu.VMEM((hp, tq, LANES), F32), pltpu.VMEM((hp, tq, dq), F32)]
        + [pltpu.VMEM((hp // ATTN_GROUPS, tq, tq), F32)] * ATTN_GROUPS,
        compiler_params=pltpu.CompilerParams(
            dimension_semantics=("arbitrary", "arbitrary", "arbitrary"),
            vmem_limit_bytes=VMEM_LIMIT),
        name="mla_attention",
    )(q, k, v1)


def _ffn_subtiles(load, store, g_ref, wgu_ref, wdn_ref):
    def prepare(i):
        h = load(i)
        return h, _rms(h, g_ref[...]).astype(BF16)

    prepared = prepare(0)
    for i in range(FFN_SUBTILES):
        h, xn = prepared
        out = h
        for ci, (lo, hi) in enumerate(FFN_CHUNK_BOUNDS):
            gate = _dot(xn, wgu_ref[:, lo:hi])
            up = _dot(xn, wgu_ref[:, D_FF + lo:D_FF + hi])
            if ci == 0 and i + 1 < FFN_SUBTILES:
                prepared = prepare(i + 1)
            act = (jax.nn.silu(gate) * up).astype(BF16)
            out = out + _dot(act, wdn_ref[lo:hi, :])
        store(i, out)


def _ffn_kernel(h_ref, g_ref, wgu_ref, wdn_ref, *rest):
    cast_srcs, o_ref, cast_dsts = _split_cast_refs(rest)
    _run_casts(cast_srcs, cast_dsts)
    rows = h_ref.shape[0] // FFN_SUBTILES

    def store(i, out):
        o_ref[i * rows:(i + 1) * rows] = out

    _ffn_subtiles(lambda i: h_ref[i * rows:(i + 1) * rows], store, g_ref, wgu_ref, wdn_ref)


def _ffn_from_time_major_kernel(h_ref, g_ref, wgu_ref, wdn_ref, *rest):
    cast_srcs, o_ref, cast_dsts = _split_cast_refs(rest)
    _run_casts(cast_srcs, cast_dsts)
    tt, nb, d = h_ref.shape
    ts = tt // FFN_SUBTILES

    def store(i, out):
        o_ref[:, i * ts:(i + 1) * ts, :] = jnp.swapaxes(out.reshape(ts, nb, d), 0, 1)

    _ffn_subtiles(lambda i: h_ref[i * ts:(i + 1) * ts].reshape(ts * nb, d), store,
                  g_ref, wgu_ref, wdn_ref)


def _proj_ffn_kernel(h_ref, a_ref, wo_ref, g_ref, wgu_ref, wdn_ref, *rest):
    cast_srcs, o_ref, cast_dsts = _split_cast_refs(rest)
    _run_casts(cast_srcs, cast_dsts)
    rows = h_ref.shape[0] // FFN_SUBTILES

    def load(i):
        sl = slice(i * rows, (i + 1) * rows)
        return h_ref[sl] + _dot(a_ref[sl], wo_ref[...])

    def store(i, out):
        o_ref[i * rows:(i + 1) * rows] = out

    _ffn_subtiles(load, store, g_ref, wgu_ref, wdn_ref)


def _ffn_call(kernel, name, n_steps, tokens, token_specs, consts, out_sds, out_spec, cast):
    cast_in, cast_out, cast_shapes = _cast_side_job(cast, n_steps)
    out = pl.pallas_call(
        kernel,
        out_shape=[out_sds] + cast_shapes,
        grid=(n_steps,),
        in_specs=list(token_specs) + [_const_spec(c.shape) for c in consts] + cast_in,
        out_specs=[out_spec] + cast_out,
        compiler_params=pltpu.CompilerParams(
            dimension_semantics=("arbitrary",), vmem_limit_bytes=VMEM_LIMIT),
        name=name,
    )(*tokens, *consts, *[stacked for stacked, _ in cast])
    return out[0], tuple(out[1:])


def _ffn_from_time_major(h, g, w_gu, w_dn, cast=()):
    seq, bsz, d = h.shape
    tt = TM_FFN // bsz
    return _ffn_call(_ffn_from_time_major_kernel, "ffn_tm", seq // tt, (h,),
                     [pl.BlockSpec((tt, bsz, d), lambda s: (s, 0, 0))], (g, w_gu, w_dn),
                     jax.ShapeDtypeStruct((bsz, seq, d), h.dtype),
                     pl.BlockSpec((bsz, tt, d), lambda s: (0, s, 0)), cast)


def _ffn(h2, g, w_gu, w_dn, cast=()):
    t, d = h2.shape
    tok = pl.BlockSpec((TM_FFN, d), lambda i: (i, 0))
    return _ffn_call(_ffn_kernel, "ffn", t // TM_FFN, (h2,), [tok], (g, w_gu, w_dn),
                     jax.ShapeDtypeStruct(h2.shape, h2.dtype), tok, cast)


def _proj_ffn(h2, attn, w_o, g, w_gu, w_dn, cast=()):
    t, d = h2.shape
    tok = pl.BlockSpec((TM_FFN, d), lambda i: (i, 0))
    return _ffn_call(_proj_ffn_kernel, "proj_ffn", t // TM_FFN, (h2, attn),
                     [tok, pl.BlockSpec((TM_FFN, attn.shape[1]), lambda i: (i, 0))],
                     (w_o, g, w_gu, w_dn), jax.ShapeDtypeStruct((t, d), h2.dtype), tok, cast)


def _row(v):
    return v.reshape(1, -1)


def _pad_lanes(v, n):
    return jnp.pad(v, [(0, 0)] * (v.ndim - 1) + [(0, n - v.shape[-1])])


def _dup(v):
    return jnp.concatenate([v, v], axis=-1)


def kernel(x, positions, mix_norm, conv_w_in, conv_w, conv_w_out, lru_w_in, lru_conv_w, lru_conv_b, lru_gate_a_w, lru_gate_a_b, lru_gate_x_w, lru_gate_x_b, lru_lambda, lru_w_out, mla_w_down, mla_q_norm, mla_kv_norm, mla_w_uq, mla_w_ukv, mla_qn_norm, mla_qr_norm, mla_kn_norm, mla_kr_norm, mla_w_o, ffn_norm, ffn_w_gu, ffn_w_down):
    bsz, seq, d = x.shape
    nh = MLA_HEADS
    half = QK_ROPE_DIM // 2

    inv_freq = ROPE_THETA ** (-jnp.arange(0, QK_ROPE_DIM, 2, dtype=F32) / QK_ROPE_DIM)
    sign_slab = _row(_pad_lanes(jnp.concatenate([-jnp.ones(half, F32), jnp.ones(half, F32)]), LANES))
    live_slab = _row(_pad_lanes(jnp.ones(QK_ROPE_DIM, F32), LANES))
    cos_tab, sin_tab = _rope_tables(positions, inv_freq, live_slab, sign_slab)
    lane_slab = jnp.arange(2 * LANES) // LANES
    pair_ones = ((lane_slab[:, None] == lane_slab[None, :]) * (1.0 / LANES)).astype(BF16)

    def mixer_jobs(i):
        kind, j = i % N_MIXERS, i // N_MIXERS
        if kind == 0:
            return ((conv_w_in, j), (conv_w_out, j))
        if kind == 1:
            return ((lru_w_in, j), (lru_w_out, j))
        return ((mla_w_o, j),)

    def ffn_jobs(i):
        return ((ffn_w_gu, i), (ffn_w_down, i))

    def cast_now(jobs):
        return tuple(stacked[index].astype(BF16) for stacked, index in jobs)

    h = x
    mixer_w = cast_now(mixer_jobs(0))
    ffn_w = None
    for i in range(DEPTH):
        kind, j = i % N_MIXERS, i // N_MIXERS
        g = _row(mix_norm[i])
        gf = _row(ffn_norm[i])
        last = i + 1 == DEPTH
        next_jobs = () if last else mixer_jobs(i + 1) + ffn_jobs(i + 1)
        to_time_major = kind == 0 and not last and (i + 1) % N_MIXERS == 1
        if ffn_w is None and not to_time_major:
            ffn_w = cast_now(ffn_jobs(i))
        if kind == 0:
            conv_args = (h, g, mixer_w[0], conv_w[j], mixer_w[1])
            if to_time_major:
                h, cast_ffn = _conv_layer_to_time_major(
                    *conv_args, cast=() if ffn_w else ffn_jobs(i))
                h, casts = _ffn(h.reshape(seq * bsz, d), gf, *(ffn_w or cast_ffn), cast=next_jobs)
                h = h.reshape(seq, bsz, d)
            else:
                h = _conv_layer(*conv_args)
                h, casts = _ffn(h.reshape(bsz * seq, d), gf, *ffn_w, cast=next_jobs)
                h = h.reshape(bsz, seq, d)
        elif kind == 1:
            wax = (0.5 * jnp.concatenate([lru_gate_a_w[j], lru_gate_x_w[j]], axis=-1)).astype(BF16)
            h = _lru_layer(h, g, mixer_w[0], lru_conv_w[j], _row(lru_conv_b[j]),
                           wax, _row(0.5 * lru_gate_a_b[j]), _row(0.5 * lru_gate_x_b[j]),
                           _row(lru_lambda[j]), mixer_w[1])
            h, casts = _ffn_from_time_major(h, gf, *ffn_w, cast=next_jobs)
        else:
            wdm = mla_w_down[j]
            wd = jnp.concatenate([wdm, wdm[:, Q_LORA_RANK + KV_LORA_RANK:]], axis=1).astype(BF16)
            wq = mla_w_uq[j].reshape(Q_LORA_RANK, nh, QK_NOPE_DIM + QK_ROPE_DIM)
            wuq = jnp.concatenate(
                [wq[:, :, :QK_NOPE_DIM].reshape(Q_LORA_RANK, nh * QK_NOPE_DIM),
                 _dup(wq[:, :, QK_NOPE_DIM:]).reshape(Q_LORA_RANK, nh * LANES)],
                axis=1).astype(BF16)
            wkv = mla_w_ukv[j].reshape(KV_LORA_RANK, nh, QK_NOPE_DIM + V_HEAD_DIM)
            wukv = jnp.concatenate(
                [wkv[:, :, :QK_NOPE_DIM].reshape(KV_LORA_RANK, nh * QK_NOPE_DIM),
                 wkv[:, :, QK_NOPE_DIM:].reshape(KV_LORA_RANK, nh * V_HEAD_DIM)], axis=1).astype(BF16)
            consts = (g, wd, _row(mla_q_norm[j]), _row(mla_kv_norm[j]), wuq, wukv,
                      _row(mla_qn_norm[j]), _row(_dup(mla_qr_norm[j])),
                      _row(mla_kn_norm[j]), _row(_dup(mla_kr_norm[j])),
                      pair_ones)
            q, k, v1 = _mla_proj(h, cos_tab, sin_tab, consts)
            attn = _attention(q, k, v1).reshape(bsz * seq, nh * V_HEAD_DIM)
            h, casts = _proj_ffn(h.reshape(bsz * seq, d), attn, mixer_w[0], gf, *ffn_w,
                                 cast=next_jobs)
            h = h.reshape(bsz, seq, d)
        if not last:
            n_mixer = len(mixer_jobs(i + 1))
            mixer_w, ffn_w = casts[:n_mixer], casts[n_mixer:]
    return h
```
